```python
import math
import jax
import jax.numpy as jnp
from jax import lax
import numpy as np

D_MODEL = 2048
BATCH = 2
SEQ = 8192
DEPTH = 4

CHUNK = 64
Q_BLOCK = 128
ROPE_THETA = 500000.0
RMS_EPS = 1e-6

N_BRANCH = 4
BRANCH_WIDTH = 512

DA_HEADS = 4
DA_QK_DIM = 64
DA_V_DIM = 2 * DA_QK_DIM
DA_ROT = DA_QK_DIM // 4

SG_GROUPS = 4
SG_DIM = 128
SG_LEN = 128

ML_HEADS = 4
ML_DIM = 128
ML_CONV = 4

MLA_HEADS = 4
MLA_Q_RANK = 512
MLA_KV_RANK = 256
MLA_NOPE = 128
MLA_ROPE = 64
MLA_V = 128
MLA_QK_DIM = MLA_NOPE + MLA_ROPE

PEER_HEADS = 8
PEER_KEY_DIM = 256
PEER_N_KEYS = 128
PEER_N_EXPERTS = PEER_N_KEYS * PEER_N_KEYS
PEER_TOPK = 16
PEER_BLOCK = 128

IN_SIZES = (
    DA_HEADS * 2 * DA_QK_DIM,
    DA_HEADS * 2 * DA_QK_DIM,
    DA_HEADS * DA_V_DIM,
    SG_GROUPS * SG_DIM,
    SG_GROUPS * SG_DIM,
    2 * ML_HEADS * ML_DIM,
    ML_HEADS * ML_DIM,
    ML_HEADS * ML_DIM,
    ML_HEADS,
    ML_HEADS,
    MLA_Q_RANK,
    MLA_KV_RANK,
    MLA_ROPE,
    N_BRANCH * D_MODEL,
)
IN_WIDTH = sum(IN_SIZES)

kernel_name = "hybrid_chunk_causal_diffattn_gmlp_mlstm_mla_peer"


def rms_norm(x, g):
    xf = x.astype(jnp.float32)
    y = xf * lax.rsqrt(jnp.mean(xf * xf, axis=-1, keepdims=True) + RMS_EPS)
    return (y * g.astype(jnp.float32)).astype(x.dtype)


def rope(x, pos, rot_dim):
    half = rot_dim // 2
    freq = ROPE_THETA ** (-jnp.arange(half, dtype=jnp.float32) / half)
    ang = pos.astype(jnp.float32)[:, :, None] * freq
    ang = ang.reshape(ang.shape[:2] + (1,) * (x.ndim - 3) + (half,))
    cos, sin = jnp.cos(ang), jnp.sin(ang)
    xf = x.astype(jnp.float32)
    x1, x2, rest = xf[..., :half], xf[..., half:rot_dim], xf[..., rot_dim:]
    out = jnp.concatenate([x1 * cos - x2 * sin, x2 * cos + x1 * sin, rest], axis=-1)
    return out.astype(x.dtype)


def query_blocks(t):
    b, s = t.shape[:2]
    t = t.reshape((b, s // Q_BLOCK, Q_BLOCK) + t.shape[2:])
    return jnp.moveaxis(t, 1, 0)


def merge_blocks(t):
    t = jnp.moveaxis(t, 0, 1)
    return t.reshape((t.shape[0], t.shape[1] * t.shape[2]) + t.shape[3:])


def chunk_mask(q0, seq):
    qi = q0 + jnp.arange(Q_BLOCK)
    ki = jnp.arange(seq)
    return (ki[None, :] // CHUNK) <= (qi[:, None] // CHUNK)


def diff_attention(q, k, v, pos, q_g, k_g, lam_vecs, out_g, lambda_init):
    seq = q.shape[1]
    q = rope(rms_norm(q, q_g), pos, DA_ROT)
    k = rope(rms_norm(k, k_g), pos, DA_ROT)
    lf = lam_vecs.astype(jnp.float32)
    lam = jnp.exp(jnp.sum(lf[0] * lf[1])) - jnp.exp(jnp.sum(lf[2] * lf[3])) + lambda_init
    scale = DA_QK_DIM ** -0.5

    def block(args):
        qb, q0 = args
        s = jnp.einsum('bqhcd,bkhcd->bhcqk', qb, k).astype(jnp.float32) * scale
        s = jnp.where(chunk_mask(q0, seq), s, -jnp.inf)
        p = jax.nn.softmax(s, axis=-1)
        w = p[:, :, 0] - lam * p[:, :, 1]
        return jnp.einsum('bhqk,bkhe->bqhe', w.astype(v.dtype), v)

    nb = seq // Q_BLOCK
    o = merge_blocks(lax.map(block, (query_blocks(q), jnp.arange(nb) * Q_BLOCK)))
    return rms_norm(o, out_g) * (1.0 - lambda_init)


def spatial_gating(u, v, v_g, w_s, b_s):
    b, s = u.shape[:2]
    v = rms_norm(v, v_g)
    idx = jnp.arange(SG_LEN)
    mask = (idx[None, :] // CHUNK) <= (idx[:, None] // CHUNK)
    w = jnp.where(mask, w_s, 0.0).astype(v.dtype)
    vb = v.reshape(b, s // SG_LEN, SG_LEN, SG_GROUPS, SG_DIM)
    sv = jnp.einsum('gij,bnjgc->bnigc', w, vb) + jnp.swapaxes(b_s, 0, 1)[None, None, :, :, None]
    return u * sv.reshape(b, s, SG_GROUPS, SG_DIM)


def causal_depthwise_conv(x, w, bias):
    width = w.shape[0]
    s = x.shape[1]
    xp = jnp.pad(x, ((0, 0), (width - 1, 0), (0, 0)))
    return sum(xp[:, j:j + s] * w[j] for j in range(width)) + bias


def mlstm(q, k, v, i_pre, f_pre, o_gate, out_g):
    b, s, h, d = q.shape
    nc = s // CHUNK

    def to_chunks(t):
        t = t.astype(jnp.float32).reshape((b, nc, CHUNK) + t.shape[2:])
        return jnp.moveaxis(jnp.moveaxis(t, 1, 0), 3, 2)

    qc = to_chunks(q)
    kc = to_chunks(k) * (d ** -0.5)
    vc = to_chunks(v)
    lic = to_chunks(i_pre)
    lfc = to_chunks(jax.nn.log_sigmoid(f_pre.astype(jnp.float32)))
    tril = jnp.tril(jnp.ones((CHUNK, CHUNK), dtype=bool))

    def step(carry, xs):
        C, n, m = carry
        qt, kt, vt, li, lf = xs
        bcum = jnp.cumsum(lf, axis=-1)
        dmat = jnp.where(tril, bcum[..., :, None] - bcum[..., None, :] + li[..., None, :], -jnp.inf)
        m_inter = bcum + m[..., None]
        m_t = jnp.maximum(m_inter, jnp.max(dmat, axis=-1))
        a_inter = jnp.exp(m_inter - m_t)
        sm = jnp.exp(dmat - m_t[..., None]) * jnp.einsum('bhtd,bhsd->bhts', qt, kt)
        num = a_inter[..., None] * jnp.einsum('bhtd,bhde->bhte', qt, C) + jnp.einsum('bhts,bhse->bhte', sm, vt)
        den = a_inter * jnp.einsum('bhtd,bhd->bht', qt, n) + jnp.sum(sm, axis=-1)
        h_t = num / jnp.maximum(jnp.abs(den), jnp.exp(-m_t))[..., None]
        g = bcum[..., -1]
        dec = g[..., None] - bcum + li
        m_new = jnp.maximum(g + m, jnp.max(dec, axis=-1))
        w_s = jnp.exp(dec - m_new[..., None])
        a = jnp.exp(g + m - m_new)
        C_new = a[..., None, None] * C + jnp.einsum('bhs,bhsd,bhse->bhde', w_s, kt, vt)
        n_new = a[..., None] * n + jnp.einsum('bhs,bhsd->bhd', w_s, kt)
        return (C_new, n_new, m_new), h_t

    init = (jnp.zeros((b, h, d, d), jnp.float32), jnp.zeros((b, h, d), jnp.float32),
            jnp.zeros((b, h), jnp.float32))
    _, hs = lax.scan(step, init, (qc, kc, vc, lic, lfc))
    hs = jnp.moveaxis(jnp.moveaxis(hs, 2, 3), 0, 1).reshape(b, s, h, d).astype(q.dtype)
    return rms_norm(o_gate * hs, out_g)


def mla(cq, ckv, k_rope, pos, q_a_g, kv_a_g, w_uq, w_ukv, qk_g):
    b, seq = cq.shape[:2]
    q = (rms_norm(cq, q_a_g) @ w_uq).reshape(b, seq, MLA_HEADS, MLA_ROPE + MLA_NOPE)
    kv = (rms_norm(ckv, kv_a_g) @ w_ukv).reshape(b, seq, MLA_HEADS, MLA_NOPE + MLA_V)
    k_nope, v = kv[..., :MLA_NOPE], kv[..., MLA_NOPE:]
    k_r = jnp.broadcast_to(k_rope[:, :, None, :], (b, seq, MLA_HEADS, MLA_ROPE))
    k = jnp.concatenate([k_r, k_nope], axis=-1)
    q = rope(rms_norm(q, qk_g[0]), pos, MLA_ROPE)
    k = rope(rms_norm(k, qk_g[1]), pos, MLA_ROPE)
    scale = MLA_QK_DIM ** -0.5

    def block(args):
        qb, q0 = args
        s = jnp.einsum('bqhd,bkhd->bhqk', qb, k).astype(jnp.float32) * scale
        s = jnp.where(chunk_mask(q0, seq), s, -jnp.inf)
        p = jax.nn.softmax(s, axis=-1)
        return jnp.einsum('bhqk,bkhe->bqhe', p.astype(v.dtype), v)

    nb = seq // Q_BLOCK
    return merge_blocks(lax.map(block, (query_blocks(q), jnp.arange(nb) * Q_BLOCK)))


def peer(h, w_q, sub_keys, u_tab, v_tab):
    b, s, d = h.shape
    q = (h @ w_q).reshape(b, s, PEER_HEADS, 2, PEER_KEY_DIM // 2)
    sc = jnp.einsum('bshcd,hckd->bshck', q, sub_keys)
    top_s, top_i = lax.top_k(sc, PEER_TOPK)
    n_cand = PEER_TOPK * PEER_TOPK
    cand_s = (top_s[..., 0, :, None] + top_s[..., 1, None, :]).reshape(b, s, PEER_HEADS, n_cand)
    cand_i = (top_i[..., 0, :, None] * PEER_N_KEYS + top_i[..., 1, None, :]).reshape(b, s, PEER_HEADS, n_cand)
    best_s, best_pos = lax.top_k(cand_s, PEER_TOPK)
    experts = jnp.take_along_axis(cand_i, best_pos, axis=-1)
    gates = jax.nn.softmax(best_s.astype(jnp.float32), axis=-1).astype(h.dtype)
    n_tok = b * s
    n_sel = PEER_HEADS * PEER_TOPK
    xs = (h.reshape(n_tok // PEER_BLOCK, PEER_BLOCK, d),
          experts.reshape(n_tok // PEER_BLOCK, PEER_BLOCK, n_sel),
          gates.reshape(n_tok // PEER_BLOCK, PEER_BLOCK, n_sel))

    def block(args):
        xb, eb, gb = args
        act = jax.nn.gelu(jnp.einsum('tkd,td->tk', u_tab[eb], xb), approximate=False) * gb
        return jnp.einsum('tk,tkd->td', act, v_tab[eb])

    return lax.map(block, xs).reshape(b, s, d)


def setup_inputs(seed: int = 0) -> dict:
    key = jax.random.key(seed)
    ks = iter(jax.random.split(key, 64))

    def nrm(shape, scale):
        return scale * jax.random.normal(next(ks), shape, jnp.float32)

    def gain(shape):
        return 1.0 + nrm(shape, 0.05)

    D = D_MODEL
    x = nrm((BATCH, SEQ, D), 1.0)
    c = nrm((BATCH, D), 1.0)
    start = CHUNK * jax.random.randint(next(ks), (BATCH, 1), 0, 64)
    positions = (start + jnp.arange(SEQ)[None, :]).astype(jnp.int32)
    f_bias = jnp.linspace(3.0, 6.0, ML_HEADS, dtype=jnp.float32)[None, :] + nrm((DEPTH, ML_HEADS), 0.1)
    ml_gate_b = jnp.stack([nrm((DEPTH, ML_HEADS), 0.1), f_bias], axis=1)
    return {
        "x": x,
        "c": c,
        "positions": positions,
        "norm1_g": gain((DEPTH, D)),
        "norm2_g": gain((DEPTH, D)),
        "w_mod": nrm((DEPTH, D, 6 * D), 0.5 * D ** -0.5),
        "b_mod": nrm((DEPTH, 6 * D), 0.02),
        "w_in": nrm((DEPTH, D, IN_WIDTH), D ** -0.5),
        "da_q_norm": gain((DEPTH, 2, DA_QK_DIM)),
        "da_k_norm": gain((DEPTH, 2, DA_QK_DIM)),
        "da_lambda": nrm((DEPTH, 4, DA_QK_DIM), 0.1),
        "da_out_norm": gain((DEPTH, DA_V_DIM)),
        "sg_v_norm": gain((DEPTH, SG_GROUPS, SG_DIM)),
        "sg_w": nrm((DEPTH, SG_GROUPS, SG_LEN, SG_LEN), 0.5 * SG_LEN ** -0.5),
        "sg_b": 1.0 + nrm((DEPTH, SG_GROUPS, SG_LEN), 0.1),
        "ml_conv_w": nrm((DEPTH, ML_CONV, 2 * ML_HEADS * ML_DIM), 0.5),
        "ml_conv_b": nrm((DEPTH, 2 * ML_HEADS * ML_DIM), 0.02),
        "ml_gate_b": ml_gate_b,
        "ml_out_norm": gain((DEPTH, ML_HEADS, ML_DIM)),
        "mla_q_norm": gain((DEPTH, MLA_Q_RANK)),
        "mla_kv_norm": gain((DEPTH, MLA_KV_RANK)),
        "mla_w_uq": nrm((DEPTH, MLA_Q_RANK, MLA_HEADS * (MLA_ROPE + MLA_NOPE)), MLA_Q_RANK ** -0.5),
        "mla_w_ukv": nrm((DEPTH, MLA_KV_RANK, MLA_HEADS * (MLA_NOPE + MLA_V)), MLA_KV_RANK ** -0.5),
        "mla_qk_norm": gain((DEPTH, 2, MLA_QK_DIM)),
        "w_branch": nrm((DEPTH, N_BRANCH, BRANCH_WIDTH, D), BRANCH_WIDTH ** -0.5),
        "w_out": nrm((DEPTH, D, D), D ** -0.5),
        "peer_w_q": nrm((DEPTH, D, PEER_HEADS * PEER_KEY_DIM), D ** -0.5),
        "peer_sub_keys": nrm((DEPTH, PEER_HEADS, 2, PEER_N_KEYS, PEER_KEY_DIM // 2), (PEER_KEY_DIM // 2) ** -0.5),
        "peer_u": nrm((DEPTH, PEER_N_EXPERTS, D), D ** -0.5),
        "peer_v": nrm((DEPTH, PEER_N_EXPERTS, D), PEER_HEADS ** -0.5),
    }


def reference(x, c, positions, norm1_g, norm2_g, w_mod, b_mod, w_in,
              da_q_norm, da_k_norm, da_lambda, da_out_norm,
              sg_v_norm, sg_w, sg_b,
              ml_conv_w, ml_conv_b, ml_gate_b, ml_out_norm,
              mla_q_norm, mla_kv_norm, mla_w_uq, mla_w_ukv, mla_qk_norm,
              w_branch, w_out,
              peer_w_q, peer_sub_keys, peer_u, peer_v):
    b, s, _ = x.shape
    cond = jax.nn.silu(c)
    split_pts = np.cumsum(IN_SIZES)[:-1].tolist()
    for l in range(DEPTH):
        mod = (cond @ w_mod[l] + b_mod[l])[:, None, :]
        sh1, sc1, g1, sh2, sc2, g2 = jnp.split(mod, 6, axis=-1)

        h = rms_norm(x, norm1_g[l]) * (1 + sc1) + sh1
        proj = h @ w_in[l]
        (da_q, da_k, da_v, sg_u, sg_v, ml_qk, ml_v, ml_o, ml_i, ml_f,
         mla_cq, mla_ckv, mla_kr, gates) = jnp.split(proj, split_pts, axis=-1)

        lambda_init = 0.8 - 0.6 * math.exp(-0.3 * l)
        ya = diff_attention(da_q.reshape(b, s, DA_HEADS, 2, DA_QK_DIM),
                            da_k.reshape(b, s, DA_HEADS, 2, DA_QK_DIM),
                            da_v.reshape(b, s, DA_HEADS, DA_V_DIM), positions,
                            da_q_norm[l], da_k_norm[l], da_lambda[l], da_out_norm[l], lambda_init)

        yb = spatial_gating(jax.nn.gelu(sg_u, approximate=False).reshape(b, s, SG_GROUPS, SG_DIM),
                            jax.nn.gelu(sg_v, approximate=False).reshape(b, s, SG_GROUPS, SG_DIM),
                            sg_v_norm[l], sg_w[l], sg_b[l])

        qk = jax.nn.silu(causal_depthwise_conv(ml_qk, ml_conv_w[l], ml_conv_b[l]))
        ml_q, ml_k = jnp.split(qk, 2, axis=-1)
        yc = mlstm(ml_q.reshape(b, s, ML_HEADS, ML_DIM), ml_k.reshape(b, s, ML_HEADS, ML_DIM),
                   ml_v.reshape(b, s, ML_HEADS, ML_DIM),
                   ml_i + ml_gate_b[l, 0], ml_f + ml_gate_b[l, 1],
                   jax.nn.sigmoid(ml_o).reshape(b, s, ML_HEADS, ML_DIM), ml_out_norm[l])

        yd = mla(mla_cq, mla_ckv, mla_kr, positions, mla_q_norm[l], mla_kv_norm[l],
                 mla_w_uq[l], mla_w_ukv[l], mla_qk_norm[l])

        branches = jnp.stack([ya.reshape(b, s, BRANCH_WIDTH), yb.reshape(b, s, BRANCH_WIDTH),
                              yc.reshape(b, s, BRANCH_WIDTH), yd.reshape(b, s, BRANCH_WIDTH)], axis=2)
        up = jnp.einsum('bsne,ned->bsnd', branches, w_branch[l])
        gate = jax.nn.sigmoid(gates.reshape(b, s, N_BRANCH, D_MODEL))
        mixed = jnp.sum(gate * up, axis=2) @ w_out[l]
        x = x + g1 * mixed

        h2 = rms_norm(x, norm2_g[l]) * (1 + sc2) + sh2
        x = x + g2 * peer(h2, peer_w_q[l], peer_sub_keys[l], peer_u[l], peer_v[l])
    return x
```

```python
import functools
import math

import numpy as np
import jax
import jax.numpy as jnp
from jax import lax
from jax.experimental import pallas as pl
from jax.experimental.pallas import tpu as pltpu

F32 = jnp.float32
BF16 = jnp.bfloat16

D_MODEL = 2048
CHUNK = 64
ROPE_THETA = 500000.0
RMS_EPS = 1e-6
N_HEADS = 4
HEAD_W = 128
BRANCH_W = N_HEADS * HEAD_W
DA_QK = 64
DA_ROT = 16
ML_CONV = 4
MLA_ROPE = 64
MLA_NOPE = 128
MLA_QK = MLA_ROPE + MLA_NOPE
MLA_PAD = 256
MLA_Q_RANK = 512
MLA_KV_RANK = 256
PEER_HEADS = 8
PEER_KEYS = 128
PEER_TOPK = 16
PEER_SEL = PEER_HEADS * PEER_TOPK

P_WIDTH = 11 * 512
COL_DAQ, COL_DAK, COL_DAV, COL_SGU, COL_SGV, COL_MLQK, COL_MLV, COL_MLO, COL_CQ, COL_MISC = 0, 1, 2, 3, 4, 5, 7, 8, 9, 10
MISC_KR = 256
MISC_GATE = 320

VMEM_LIMIT = 56 * 1024 * 1024
NEG_INF = float("-inf")


def _cparams(*sem):
    return pltpu.CompilerParams(dimension_semantics=sem, vmem_limit_bytes=VMEM_LIMIT)


def _gelu(x):
    return 0.5 * x * (1.0 + lax.erf(x * (1.0 / math.sqrt(2.0))))


def _dot_nt(a, b):
    return lax.dot_general(a, b, (((1,), (1,)), ((), ())), preferred_element_type=F32)


def _mod_kernel(c_ref, w_ref, b_ref, o_ref):
    c = c_ref[...]
    cond = c * jax.nn.sigmoid(c)
    o_ref[0] = jnp.dot(cond, w_ref[0], preferred_element_type=F32,
                       precision=lax.Precision.HIGHEST) + b_ref[0]


def _modulation(c, w_mod, b_mod):
    depth, d, w6 = w_mod.shape
    b = c.shape[0]
    rows = 8
    cp = jnp.zeros((rows, d), F32).at[:b].set(c)
    tn = 1024
    out = pl.pallas_call(
        _mod_kernel,
        out_shape=jax.ShapeDtypeStruct((depth, rows, w6), F32),
        grid=(depth, w6 // tn),
        in_specs=[pl.BlockSpec((rows, d), lambda l, j: (0, 0)),
                  pl.BlockSpec((1, d, tn), lambda l, j: (l, 0, j)),
                  pl.BlockSpec((1, 1, tn), lambda l, j: (l, 0, j))],
        out_specs=pl.BlockSpec((1, rows, tn), lambda l, j: (l, 0, j)),
        compiler_params=_cparams("parallel", "parallel"),
        name="modulation",
    )(cp, w_mod, b_mod.reshape(depth, 1, w6))
    return out[:, :b].reshape(depth, b, 6, d)


def _nm_kernel(x_ref, g_ref, mod_ref, w_ref, *rest, sc_row, sh_row, emit_h):
    if emit_h:
        o_ref, h_out_ref, h_ref = rest
    else:
        o_ref, h_ref = rest

    @pl.when(pl.program_id(1) == 0)
    def _():
        x = x_ref[...]
        ms = jnp.mean(x * x, axis=-1, keepdims=True)
        y = x * lax.rsqrt(ms + RMS_EPS) * g_ref[...]
        h = y * (1.0 + mod_ref[sc_row:sc_row + 1, :]) + mod_ref[sh_row:sh_row + 1, :]
        h_ref[...] = h.astype(BF16)
        if emit_h:
            h_out_ref[...] = h.astype(BF16)

    o_ref[...] = jnp.dot(h_ref[...], w_ref[...], preferred_element_type=F32).astype(o_ref.dtype)


def _norm_matmul(x, g, mod, w, *, seq, sc_row, sh_row, out_dtype, emit_h=False, tm=512, tn=512, name):
    n, d = x.shape
    wn = w.shape[1]
    out_shape = [jax.ShapeDtypeStruct((n, wn), out_dtype)]
    out_specs = [pl.BlockSpec((tm, tn), lambda i, j: (i, j))]
    if emit_h:
        out_shape.append(jax.ShapeDtypeStruct((n, d), BF16))
        out_specs.append(pl.BlockSpec((tm, d), lambda i, j: (i, 0)))
    res = pl.pallas_call(
        functools.partial(_nm_kernel, sc_row=sc_row, sh_row=sh_row, emit_h=emit_h),
        out_shape=out_shape,
        grid=(n // tm, wn // tn),
        in_specs=[pl.BlockSpec((tm, d), lambda i, j: (i, 0)),
                  pl.BlockSpec((1, d), lambda i, j: (0, 0)),
                  pl.BlockSpec((None, 6, d), lambda i, j: (i * tm // seq, 0, 0)),
                  pl.BlockSpec((d, tn), lambda i, j: (0, j))],
        out_specs=out_specs,
        scratch_shapes=[pltpu.VMEM((tm, d), BF16)],
        compiler_params=_cparams("parallel", "arbitrary"),
        name=name,
    )(x, g.reshape(1, d), mod, w)
    return res if emit_h else res[0]


def _rope_table_kernel(pos_ref, f_ref, cos_ref, sin_ref):
    ang = pos_ref[...] * f_ref[0:1, :]
    cos_ref[...] = jnp.cos(ang)
    sin_ref[...] = jnp.sin(ang) * f_ref[1:2, :]


def _rope_tables(pos_col, freq_sign, tm=1024):
    n = pos_col.shape[0]
    return pl.pallas_call(
        _rope_table_kernel,
        out_shape=[jax.ShapeDtypeStruct((n, 128), F32)] * 2,
        grid=(n // tm,),
        in_specs=[pl.BlockSpec((tm, 1), lambda i: (i, 0)),
                  pl.BlockSpec((2, 128), lambda i: (0, 0))],
        out_specs=[pl.BlockSpec((tm, 128), lambda i: (i, 0))] * 2,
        compiler_params=_cparams("parallel"),
        name="rope_tables",
    )(pos_col, freq_sign)


def _lane_freq_sign(group, rot):
    half = rot // 2
    freq = ROPE_THETA ** (-jnp.arange(half, dtype=F32) / half)
    lane = np.arange(128) % group
    idx = np.where(lane < rot, lane % half, 0)
    f = jnp.where(jnp.asarray(lane < rot), freq[idx], 0.0)
    sign = np.where(lane < half, -1.0, np.where(lane < rot, 1.0, 0.0)).astype(np.float32)
    return jnp.stack([f, jnp.asarray(sign)], axis=0)


def _rope_apply(x, cos, sin, group, rot):
    half = rot // 2
    lane = lax.broadcasted_iota(jnp.int32, (1, 128), 1) % group
    partner = jnp.where(lane < half, pltpu.roll(x, 128 - half, 1), pltpu.roll(x, half, 1))
    return x * cos + partner * sin


def _da_prep_kernel(q_ref, k_ref, cos_ref, sin_ref, qg_ref, kg_ref, qo_ref, ko_ref):
    cos = cos_ref[...]
    sin = sin_ref[...]
    lo = lax.broadcasted_iota(jnp.int32, (1, 128), 1) < DA_QK

    def prep(ref, g_ref, o_ref, scale):
        for h in range(N_HEADS):
            x = ref[:, h * HEAD_W:(h + 1) * HEAD_W]
            sq = x * x
            s_lo = jnp.sum(jnp.where(lo, sq, 0.0), axis=-1, keepdims=True)
            s_hi = jnp.sum(jnp.where(lo, 0.0, sq), axis=-1, keepdims=True)
            ms = jnp.where(lo, s_lo, s_hi) * (1.0 / DA_QK)
            y = x * lax.rsqrt(ms + RMS_EPS) * g_ref[...]
            y = _rope_apply(y, cos, sin, DA_QK, DA_ROT)
            o_ref[:, h * HEAD_W:(h + 1) * HEAD_W] = (y * scale).astype(BF16)

    prep(q_ref, qg_ref, qo_ref, DA_QK ** -0.5)
    prep(k_ref, kg_ref, ko_ref, 1.0)


def _da_prep(p, cos, sin, q_g, k_g, tm=512):
    n = p.shape[0]
    return pl.pallas_call(
        _da_prep_kernel,
        out_shape=[jax.ShapeDtypeStruct((n, BRANCH_W), BF16)] * 2,
        grid=(n // tm,),
        in_specs=[pl.BlockSpec((tm, BRANCH_W), lambda i: (i, COL_DAQ)),
                  pl.BlockSpec((tm, BRANCH_W), lambda i: (i, COL_DAK)),
                  pl.BlockSpec((tm, 128), lambda i: (i, 0)),
                  pl.BlockSpec((tm, 128), lambda i: (i, 0)),
                  pl.BlockSpec((1, 128), lambda i: (0, 0)),
                  pl.BlockSpec((1, 128), lambda i: (0, 0))],
        out_specs=[pl.BlockSpec((tm, BRANCH_W), lambda i: (i, 0))] * 2,
        compiler_params=_cparams("parallel"),
        name="da_prep",
    )(p, p, cos, sin, q_g.reshape(1, 128), k_g.reshape(1, 128))


def _causal_pairs(nq, tq, tk):
    qi, ki, last = [], [], []
    for q in range(nq):
        nk = -(-((q + 1) * tq) // tk)
        for k in range(nk):
            qi.append(q)
            ki.append(k)
            last.append(int(k == nk - 1))
    return (jnp.asarray(qi, jnp.int32), jnp.asarray(ki, jnp.int32), jnp.asarray(last, jnp.int32))


def _softmax_step(s, v, idx, m_ref, l_ref, acc_ref):
    m_prev = m_ref[idx]
    m_new = jnp.maximum(m_prev, jnp.max(s, axis=-1, keepdims=True))
    alpha = jnp.exp(m_prev - m_new)
    p = jnp.exp(s - m_new)
    l_ref[idx] = alpha * l_ref[idx] + jnp.sum(p, axis=-1, keepdims=True)
    acc_ref[idx] = alpha * acc_ref[idx] + jnp.dot(p.astype(BF16), v, preferred_element_type=F32)
    m_ref[idx] = m_new


def _chunk_mask(qi, ki, tq, tk):
    row = qi * tq + lax.broadcasted_iota(jnp.int32, (tq, 1), 0)
    col = ki * tk + lax.broadcasted_iota(jnp.int32, (1, tk), 1)
    return (col // CHUNK) <= (row // CHUNK)


def _da_attn_kernel(qi_ref, ki_ref, last_ref, q_ref, k_ref, v_ref, lam_ref, og_ref, o_ref,
                    m_ref, l_ref, acc_ref, *, lambda_init, tq, tk):
    p_id = pl.program_id(1)
    qi = qi_ref[p_id]
    ki = ki_ref[p_id]

    @pl.when(ki == 0)
    def _():
        m_ref[...] = jnp.full(m_ref.shape, NEG_INF, F32)
        l_ref[...] = jnp.zeros(l_ref.shape, F32)
        acc_ref[...] = jnp.zeros(acc_ref.shape, F32)

    mask = _chunk_mask(qi, ki, tq, tk)
    lo = lax.broadcasted_iota(jnp.int32, (1, HEAD_W), 1) < DA_QK
    for h in range(N_HEADS):
        sl = slice(h * HEAD_W, (h + 1) * HEAD_W)
        q = q_ref[:, sl]
        k = k_ref[:, sl]
        v = v_ref[:, sl].astype(BF16)
        for c in range(2):
            qc = jnp.where(lo, q, jnp.zeros_like(q)) if c == 0 else jnp.where(lo, jnp.zeros_like(q), q)
            s = jnp.where(mask, _dot_nt(qc, k), NEG_INF)
            _softmax_step(s, v, 2 * h + c, m_ref, l_ref, acc_ref)

    @pl.when(last_ref[p_id] == 1)
    def _():
        lf = lam_ref[...]
        lam = (jnp.exp(jnp.sum(lf[0:1] * lf[1:2], axis=-1, keepdims=True))
               - jnp.exp(jnp.sum(lf[2:3] * lf[3:4], axis=-1, keepdims=True)) + lambda_init)
        for h in range(N_HEADS):
            o = acc_ref[2 * h] / l_ref[2 * h] - lam * (acc_ref[2 * h + 1] / l_ref[2 * h + 1])
            ms = jnp.mean(o * o, axis=-1, keepdims=True)
            y = o * lax.rsqrt(ms + RMS_EPS) * og_ref[...] * (1.0 - lambda_init)
            o_ref[:, h * HEAD_W:(h + 1) * HEAD_W] = y.astype(o_ref.dtype)


def _da_attn(qn, kn, p, lam_vecs, out_g, *, batch, seq, lambda_init, tq=512, tk=512):
    n = qn.shape[0]
    nq, nk = seq // tq, seq // tk
    qi, ki, last = _causal_pairs(nq, tq, tk)
    grid_spec = pltpu.PrefetchScalarGridSpec(
        num_scalar_prefetch=3,
        grid=(batch, int(qi.shape[0])),
        in_specs=[pl.BlockSpec((tq, BRANCH_W), lambda b, p_, qi_, ki_, la_: (b * nq + qi_[p_], 0)),
                  pl.BlockSpec((tk, BRANCH_W), lambda b, p_, qi_, ki_, la_: (b * nk + ki_[p_], 0)),
                  pl.BlockSpec((tk, BRANCH_W), lambda b, p_, qi_, ki_, la_: (b * nk + ki_[p_], COL_DAV)),
                  pl.BlockSpec((4, DA_QK), lambda b, p_, qi_, ki_, la_: (0, 0)),
                  pl.BlockSpec((1, HEAD_W), lambda b, p_, qi_, ki_, la_: (0, 0))],
        out_specs=pl.BlockSpec((tq, BRANCH_W), lambda b, p_, qi_, ki_, la_: (b * nq + qi_[p_], 0)),
        scratch_shapes=[pltpu.VMEM((2 * N_HEADS, tq, 1), F32),
                        pltpu.VMEM((2 * N_HEADS, tq, 1), F32),
                        pltpu.VMEM((2 * N_HEADS, tq, HEAD_W), F32)],
    )
    return pl.pallas_call(
        functools.partial(_da_attn_kernel, lambda_init=lambda_init, tq=tq, tk=tk),
        out_shape=jax.ShapeDtypeStruct((n, BRANCH_W), BF16),
        grid_spec=grid_spec,
        compiler_params=_cparams("parallel", "arbitrary"),
        name="da_attn",
    )(qi, ki, last, qn, kn, p, lam_vecs, out_g.reshape(1, HEAD_W))


def _sg_kernel(u_ref, v_ref, vg_ref, w_ref, b_ref, o_ref, *, tm):
    idx_i = lax.broadcasted_iota(jnp.int32, (128, 128), 0)
    idx_j = lax.broadcasted_iota(jnp.int32, (128, 128), 1)
    wmask = (idx_j // CHUNK) <= (idx_i // CHUNK)
    for g in range(N_HEADS):
        sl = slice(g * HEAD_W, (g + 1) * HEAD_W)
        u = _gelu(u_ref[:, sl])
        v = _gelu(v_ref[:, sl])
        ms = jnp.mean(v * v, axis=-1, keepdims=True)
        vn = (v * lax.rsqrt(ms + RMS_EPS) * vg_ref[g:g + 1, :]).astype(BF16)
        w = jnp.where(wmask, w_ref[g], 0.0).astype(BF16)
        bias = b_ref[:, g:g + 1]
        for nb in range(tm // 128):
            rows = slice(nb * 128, (nb + 1) * 128)
            sv = jnp.dot(w, vn[rows], preferred_element_type=F32) + bias
            o_ref[rows, sl] = (u[rows] * sv).astype(o_ref.dtype)


def _spatial_gating(p, v_g, w_s, b_s, tm=512):
    n = p.shape[0]
    return pl.pallas_call(
        functools.partial(_sg_kernel, tm=tm),
        out_shape=jax.ShapeDtypeStruct((n, BRANCH_W), BF16),
        grid=(n // tm,),
        in_specs=[pl.BlockSpec((tm, BRANCH_W), lambda i: (i, COL_SGU)),
                  pl.BlockSpec((tm, BRANCH_W), lambda i: (i, COL_SGV)),
                  pl.BlockSpec((N_HEADS, 128), lambda i: (0, 0)),
                  pl.BlockSpec((N_HEADS, 128, 128), lambda i: (0, 0, 0)),
                  pl.BlockSpec((128, N_HEADS), lambda i: (0, 0))],
        out_specs=pl.BlockSpec((tm, BRANCH_W), lambda i: (i, 0)),
        compiler_params=_cparams("parallel"),
        name="spatial_gating",
    )(p, p, v_g, w_s, b_s.T)


def _ml_conv_kernel(x_ref, halo_ref, w_ref, b_ref, o_ref, *, tm, seq):
    i = pl.program_id(0)
    first = (i * tm) % seq == 0
    halo = jnp.where(first, 0.0, halo_ref[...])
    xc = jnp.concatenate([halo, x_ref[...]], axis=0)
    acc = jnp.zeros((tm, x_ref.shape[1]), F32) + b_ref[...]
    for j in range(ML_CONV):
        shift = ML_CONV - 1 - j
        xs = xc if shift == 0 else pltpu.roll(xc, shift, 0)
        acc = acc + xs[8:8 + tm] * w_ref[j:j + 1, :]
    o_ref[...] = acc * jax.nn.sigmoid(acc)


def _ml_conv(p, conv_w, conv_b, *, seq, tm=512):
    n = p.shape[0]
    width = 2 * BRANCH_W
    cb = width // BRANCH_W
    return pl.pallas_call(
        functools.partial(_ml_conv_kernel, tm=tm, seq=seq),
        out_shape=jax.ShapeDtypeStruct((n, width), F32),
        grid=(n // tm, cb),
        in_specs=[pl.BlockSpec((tm, BRANCH_W), lambda i, j: (i, COL_MLQK + j)),
                  pl.BlockSpec((8, BRANCH_W), lambda i, j: (jnp.maximum(i * (tm // 8) - 1, 0), COL_MLQK + j)),
                  pl.BlockSpec((ML_CONV, BRANCH_W), lambda i, j: (0, j)),
                  pl.BlockSpec((1, BRANCH_W), lambda i, j: (0, j))],
        out_specs=pl.BlockSpec((tm, BRANCH_W), lambda i, j: (i, j)),
        compiler_params=_cparams("parallel", "parallel"),
        name="ml_conv",
    )(p, p, conv_w, conv_b.reshape(1, width))


def _log_sigmoid(x):
    return jnp.minimum(x, 0.0) - jnp.log1p(jnp.exp(-jnp.abs(x)))


def _mlstm_kernel(q_ref, k_ref, v_ref, o_ref, gc_ref, gr_ref, bc_ref, br_ref, og_ref, y_ref,
                  c_ref, n_ref, m_ref):
    @pl.when(pl.program_id(1) == 0)
    def _():
        c_ref[...] = jnp.zeros(c_ref.shape, F32)
        n_ref[...] = jnp.zeros(n_ref.shape, F32)
        m_ref[...] = jnp.zeros(m_ref.shape, F32)

    L = CHUNK
    it = lax.broadcasted_iota(jnp.int32, (L, L), 0)
    is_ = lax.broadcasted_iota(jnp.int32, (L, L), 1)
    tril = is_ <= it
    gcol = gc_ref[:, MISC_GATE:MISC_GATE + 2 * N_HEADS] + bc_ref[...]
    grow = gr_ref[...] + br_ref[...]
    for h in range(N_HEADS):
        sl = slice(h * HEAD_W, (h + 1) * HEAD_W)
        q = q_ref[:, sl].astype(BF16)
        k = (k_ref[:, sl] * (HEAD_W ** -0.5))
        kb = k.astype(BF16)
        v = v_ref[:, sl]
        vb = v.astype(BF16)
        li_row = grow[h:h + 1, :]
        lf_row = _log_sigmoid(grow[N_HEADS + h:N_HEADS + h + 1, :])
        li_col = gcol[:, h:h + 1]
        lf_col = _log_sigmoid(gcol[:, N_HEADS + h:N_HEADS + h + 1])
        bcum_col = jnp.sum(jnp.where(tril, lf_row, 0.0), axis=1, keepdims=True)
        bcum_row = jnp.sum(jnp.where(tril, 0.0, lf_col) + jnp.where(is_ == it, lf_col, 0.0),
                           axis=0, keepdims=True)
        g = jnp.sum(lf_row, axis=1, keepdims=True)
        m_prev = m_ref[h][:, 0:1]
        dmat = jnp.where(tril, bcum_col - bcum_row + li_row, NEG_INF)
        m_inter = bcum_col + m_prev
        m_t = jnp.maximum(m_inter, jnp.max(dmat, axis=1, keepdims=True))
        a_inter = jnp.exp(m_inter - m_t)
        sm = jnp.exp(dmat - m_t) * _dot_nt(q, kb)
        num = (a_inter * jnp.dot(q, c_ref[h].astype(BF16), preferred_element_type=F32)
               + jnp.dot(sm.astype(BF16), vb, preferred_element_type=F32))
        qn = jnp.sum(q.astype(F32) * n_ref[h], axis=1, keepdims=True)
        den = a_inter * qn + jnp.sum(sm, axis=1, keepdims=True)
        h_t = num / jnp.maximum(jnp.abs(den), jnp.exp(-m_t))
        dec_row = g - bcum_row + li_row
        dec_col = g - bcum_col + li_col
        m_new = jnp.maximum(g + m_prev, jnp.max(dec_row, axis=1, keepdims=True))
        w_col = jnp.exp(dec_col - m_new)
        a = jnp.exp(g + m_prev - m_new)
        kw = w_col * k
        c_ref[h] = a * c_ref[h] + lax.dot_general(kw.astype(BF16), vb, (((0,), (0,)), ((), ())),
                                                  preferred_element_type=F32)
        n_ref[h] = a * n_ref[h] + jnp.sum(kw, axis=0, keepdims=True)
        m_ref[h] = jnp.broadcast_to(m_new, (1, HEAD_W))
        y = jax.nn.sigmoid(o_ref[:, sl]) * h_t
        ms = jnp.mean(y * y, axis=-1, keepdims=True)
        y_ref[:, sl] = (y * lax.rsqrt(ms + RMS_EPS) * og_ref[h:h + 1, :]).astype(y_ref.dtype)


def _mlstm(qk, p, gate_b, out_g, *, batch, seq):
    n = p.shape[0]
    nc = seq // CHUNK
    g8 = p[:, COL_MISC * 512 + MISC_GATE:COL_MISC * 512 + MISC_GATE + 2 * N_HEADS]
    grow = jnp.swapaxes(g8.reshape(batch, nc, CHUNK, 2 * N_HEADS), 2, 3)
    bias = gate_b.reshape(1, 2 * N_HEADS)
    tok = lambda b, c: (b * nc + c, 0)
    return pl.pallas_call(
        _mlstm_kernel,
        out_shape=jax.ShapeDtypeStruct((n, BRANCH_W), BF16),
        grid=(batch, nc),
        in_specs=[pl.BlockSpec((CHUNK, BRANCH_W), lambda b, c: (b * nc + c, 0)),
                  pl.BlockSpec((CHUNK, BRANCH_W), lambda b, c: (b * nc + c, 1)),
                  pl.BlockSpec((CHUNK, BRANCH_W), lambda b, c: (b * nc + c, COL_MLV)),
                  pl.BlockSpec((CHUNK, BRANCH_W), lambda b, c: (b * nc + c, COL_MLO)),
                  pl.BlockSpec((CHUNK, BRANCH_W), lambda b, c: (b * nc + c, COL_MISC)),
                  pl.BlockSpec((None, None, 2 * N_HEADS, CHUNK), lambda b, c: (b, c, 0, 0)),
                  pl.BlockSpec((1, 2 * N_HEADS), lambda b, c: (0, 0)),
                  pl.BlockSpec((2 * N_HEADS, 1), lambda b, c: (0, 0)),
                  pl.BlockSpec((N_HEADS, HEAD_W), lambda b, c: (0, 0))],
        out_specs=pl.BlockSpec((CHUNK, BRANCH_W), tok),
        scratch_shapes=[pltpu.VMEM((N_HEADS, HEAD_W, HEAD_W), F32),
                        pltpu.VMEM((N_HEADS, 1, HEAD_W), F32),
                        pltpu.VMEM((N_HEADS, 1, HEAD_W), F32)],
        compiler_params=_cparams("parallel", "arbitrary"),
        name="mlstm",
    )(qk, qk, p, p, p, grow, bias, bias.reshape(2 * N_HEADS, 1), out_g)


def _mla_prep_kernel(cq_ref, misc_ref, cos_ref, sin_ref, qa_ref, kva_ref, wq_ref, wk_ref, wv_ref, qkg_ref,
                     qo_ref, ko_ref, vo_ref):
    def rms(x, g):
        ms = jnp.mean(x * x, axis=-1, keepdims=True)
        return x * lax.rsqrt(ms + RMS_EPS) * g

    cqn = rms(cq_ref[...], qa_ref[...]).astype(BF16)
    ckvn = rms(misc_ref[:, :MLA_KV_RANK], kva_ref[...]).astype(BF16)
    lane = lax.broadcasted_iota(jnp.int32, (1, 128), 1)
    kr = jnp.where(lane < MLA_ROPE, misc_ref[:, MISC_KR:MISC_KR + 128], 0.0).astype(BF16)
    q_lin = jnp.dot(cqn, wq_ref[...], preferred_element_type=F32)
    k_lin = jnp.dot(jnp.concatenate([ckvn, kr], axis=1), wk_ref[...], preferred_element_type=F32)
    vo_ref[...] = jnp.dot(ckvn, wv_ref[...], preferred_element_type=F32).astype(vo_ref.dtype)
    cos = cos_ref[...]
    sin = sin_ref[...]

    def finish(x_lin, g_row, o_ref, scale):
        for h in range(N_HEADS):
            x = x_lin[:, h * MLA_PAD:(h + 1) * MLA_PAD]
            ms = jnp.sum(x * x, axis=-1, keepdims=True) * (1.0 / MLA_QK)
            y = x * lax.rsqrt(ms + RMS_EPS) * qkg_ref[g_row:g_row + 1, :]
            y0 = _rope_apply(y[:, :128], cos, sin, 128, MLA_ROPE)
            o_ref[:, h * MLA_PAD:h * MLA_PAD + 128] = (y0 * scale).astype(o_ref.dtype)
            o_ref[:, h * MLA_PAD + 128:(h + 1) * MLA_PAD] = (y[:, 128:] * scale).astype(o_ref.dtype)

    finish(q_lin, 0, qo_ref, MLA_QK ** -0.5)
    finish(k_lin, 1, ko_ref, 1.0)


def _mla_prep(p, cos, sin, q_a_g, kv_a_g, wq, wk, wv, qk_g, tm=512):
    n = p.shape[0]
    hw = N_HEADS * MLA_PAD
    full = lambda shape: pl.BlockSpec(shape, lambda i: (0,) * len(shape))
    return pl.pallas_call(
        _mla_prep_kernel,
        out_shape=[jax.ShapeDtypeStruct((n, hw), BF16), jax.ShapeDtypeStruct((n, hw), BF16),
                   jax.ShapeDtypeStruct((n, BRANCH_W), BF16)],
        grid=(n // tm,),
        in_specs=[pl.BlockSpec((tm, 512), lambda i: (i, COL_CQ)),
                  pl.BlockSpec((tm, 512), lambda i: (i, COL_MISC)),
                  pl.BlockSpec((tm, 128), lambda i: (i, 0)),
                  pl.BlockSpec((tm, 128), lambda i: (i, 0)),
                  full((1, MLA_Q_RANK)), full((1, MLA_KV_RANK)),
                  full(wq.shape), full(wk.shape), full(wv.shape), full((2, MLA_PAD))],
        out_specs=[pl.BlockSpec((tm, hw), lambda i: (i, 0)), pl.BlockSpec((tm, hw), lambda i: (i, 0)),
                   pl.BlockSpec((tm, BRANCH_W), lambda i: (i, 0))],
        compiler_params=_cparams("parallel"),
        name="mla_prep",
    )(p, p, cos, sin, q_a_g.reshape(1, -1), kv_a_g.reshape(1, -1), wq, wk, wv, qk_g)


def _mla_attn_kernel(qi_ref, ki_ref, last_ref, q_ref, k_ref, v_ref, o_ref, m_ref, l_ref, acc_ref, *, tq, tk):
    p_id = pl.program_id(1)
    qi = qi_ref[p_id]
    ki = ki_ref[p_id]

    @pl.when(ki == 0)
    def _():
        m_ref[...] = jnp.full(m_ref.shape, NEG_INF, F32)
        l_ref[...] = jnp.zeros(l_ref.shape, F32)
        acc_ref[...] = jnp.zeros(acc_ref.shape, F32)

    mask = _chunk_mask(qi, ki, tq, tk)
    for h in range(N_HEADS):
        q = q_ref[:, h * MLA_PAD:(h + 1) * MLA_PAD]
        k = k_ref[:, h * MLA_PAD:(h + 1) * MLA_PAD]
        v = v_ref[:, h * HEAD_W:(h + 1) * HEAD_W]
        s = jnp.where(mask, _dot_nt(q, k), NEG_INF)
        _softmax_step(s, v, h, m_ref, l_ref, acc_ref)

    @pl.when(last_ref[p_id] == 1)
    def _():
        for h in range(N_HEADS):
            o_ref[:, h * HEAD_W:(h + 1) * HEAD_W] = (acc_ref[h] / l_ref[h]).astype(o_ref.dtype)


def _mla_attn(qm, km, vm, *, batch, seq, tq=512, tk=512):
    n = qm.shape[0]
    hw = N_HEADS * MLA_PAD
    nq, nk = seq // tq, seq // tk
    qi, ki, last = _causal_pairs(nq, tq, tk)
    grid_spec = pltpu.PrefetchScalarGridSpec(
        num_scalar_prefetch=3,
        grid=(batch, int(qi.shape[0])),
        in_specs=[pl.BlockSpec((tq, hw), lambda b, p_, qi_, ki_, la_: (b * nq + qi_[p_], 0)),
                  pl.BlockSpec((tk, hw), lambda b, p_, qi_, ki_, la_: (b * nk + ki_[p_], 0)),
                  pl.BlockSpec((tk, BRANCH_W), lambda b, p_, qi_, ki_, la_: (b * nk + ki_[p_], 0))],
        out_specs=pl.BlockSpec((tq, BRANCH_W), lambda b, p_, qi_, ki_, la_: (b * nq + qi_[p_], 0)),
        scratch_shapes=[pltpu.VMEM((N_HEADS, tq, 1), F32),
                        pltpu.VMEM((N_HEADS, tq, 1), F32),
                        pltpu.VMEM((N_HEADS, tq, HEAD_W), F32)],
    )
    return pl.pallas_call(
        functools.partial(_mla_attn_kernel, tq=tq, tk=tk),
        out_shape=jax.ShapeDtypeStruct((n, BRANCH_W), BF16),
        grid_spec=grid_spec,
        compiler_params=_cparams("parallel", "arbitrary"),
        name="mla_attn",
    )(qi, ki, last, qm, km, vm)


def _merge_kernel(ya_ref, yb_ref, yc_ref, yd_ref, g0_ref, g1_ref, g2_ref, g3_ref, wb_ref, o_ref):
    acc = None
    for n, (y_ref, g_ref) in enumerate(((ya_ref, g0_ref), (yb_ref, g1_ref), (yc_ref, g2_ref), (yd_ref, g3_ref))):
        up = jnp.dot(y_ref[...], wb_ref[n], preferred_element_type=F32)
        term = jax.nn.sigmoid(g_ref[...].astype(F32)) * up
        acc = term if acc is None else acc + term
    o_ref[...] = acc.astype(o_ref.dtype)


def _merge(ys, gates, wb, tm=1024, tn=512):
    n = gates.shape[0]
    nj = D_MODEL // tn
    y_spec = pl.BlockSpec((tm, BRANCH_W), lambda i, j: (i, 0))
    g_specs = [pl.BlockSpec((tm, tn), functools.partial(lambda i, j, nn: (i, nn * nj + j), nn=nn)) for nn in range(4)]
    return pl.pallas_call(
        _merge_kernel,
        out_shape=jax.ShapeDtypeStruct((n, D_MODEL), BF16),
        grid=(n // tm, nj),
        in_specs=[y_spec] * 4 + g_specs + [pl.BlockSpec((4, BRANCH_W, tn), lambda i, j: (0, 0, j))],
        out_specs=pl.BlockSpec((tm, tn), lambda i, j: (i, j)),
        compiler_params=_cparams("parallel", "arbitrary"),
        name="merge",
    )(*ys, gates, gates, gates, gates, wb)


def _resid_matmul_kernel(m_ref, w_ref, x_ref, mod_ref, o_ref, *, gate_row):
    y = jnp.dot(m_ref[...], w_ref[...], preferred_element_type=F32)
    o_ref[...] = x_ref[...] + mod_ref[gate_row:gate_row + 1, :] * y


def _resid_matmul(m, w, x, mod, *, seq, gate_row, tm=1024, tn=512):
    n, d = x.shape
    kdim = m.shape[1]
    return pl.pallas_call(
        functools.partial(_resid_matmul_kernel, gate_row=gate_row),
        out_shape=jax.ShapeDtypeStruct((n, d), F32),
        grid=(n // tm, d // tn),
        in_specs=[pl.BlockSpec((tm, kdim), lambda i, j: (i, 0)),
                  pl.BlockSpec((kdim, tn), lambda i, j: (0, j)),
                  pl.BlockSpec((tm, tn), lambda i, j: (i, j)),
                  pl.BlockSpec((None, 6, tn), lambda i, j: (i * tm // seq, 0, j))],
        out_specs=pl.BlockSpec((tm, tn), lambda i, j: (i, j)),
        compiler_params=_cparams("parallel", "arbitrary"),
        name="out_proj",
    )(m, w, x, mod)


def _top1_rounds(s, ids, rounds):
    nrow = s.shape[0]
    row = lax.broadcasted_iota(jnp.int32, s.shape, 0)
    vals, outs = [], []
    for _ in range(rounds):
        m = jnp.max(s, axis=0, keepdims=True)
        pos = jnp.min(jnp.where(s == m, row, nrow), axis=0, keepdims=True)
        sel = row == pos
        vals.append(m)
        outs.append(jnp.sum(jnp.where(sel, ids, 0), axis=0, keepdims=True))
        s = jnp.where(sel, NEG_INF, s)
    return jnp.concatenate(vals, axis=0), jnp.concatenate(outs, axis=0)


def _route_kernel(q_ref, keys_ref, e_ref, g_ref, *, tt):
    krow = lax.broadcasted_iota(jnp.int32, (PEER_KEYS, 128), 0)
    for sub in range(tt // 128):
        rows = slice(sub * 128, (sub + 1) * 128)
        tops = []
        for c in range(2):
            qc = q_ref[rows, c * 128:(c + 1) * 128]
            sc = lax.dot_general(keys_ref[c], qc, (((1,), (1,)), ((), ())), preferred_element_type=F32,
                                 precision=lax.Precision.HIGHEST)
            tops.append(_top1_rounds(sc, krow, PEER_TOPK))
        (s0, i0), (s1, i1) = tops
        cand_s = jnp.concatenate([s0[a:a + 1] + s1 for a in range(PEER_TOPK)], axis=0)
        cand_i = jnp.concatenate([i0[a:a + 1] * PEER_KEYS + i1 for a in range(PEER_TOPK)], axis=0)
        best_s, experts = _top1_rounds(cand_s, cand_i, PEER_TOPK)
        ex = jnp.exp(best_s - jnp.max(best_s, axis=0, keepdims=True))
        e_ref[:, rows] = experts
        g_ref[:, rows] = ex / jnp.sum(ex, axis=0, keepdims=True)


def _peer_route(q, sub_keys, tt=256):
    n = q.shape[0]
    e, g = pl.pallas_call(
        functools.partial(_route_kernel, tt=tt),
        out_shape=[jax.ShapeDtypeStruct((PEER_HEADS, PEER_TOPK, n), jnp.int32),
                   jax.ShapeDtypeStruct((PEER_HEADS, PEER_TOPK, n), F32)],
        grid=(n // tt, PEER_HEADS),
        in_specs=[pl.BlockSpec((tt, 256), lambda i, h: (i, h)),
                  pl.BlockSpec((None, 2, PEER_KEYS, 128), lambda i, h: (h, 0, 0, 0))],
        out_specs=[pl.BlockSpec((None, PEER_TOPK, tt), lambda i, h: (h, 0, i)),
                   pl.BlockSpec((None, PEER_TOPK, tt), lambda i, h: (h, 0, i))],
        compiler_params=_cparams("parallel", "parallel"),
        name="peer_route",
    )(q, sub_keys)
    return e.reshape(PEER_SEL, n), g.reshape(PEER_SEL, n)


PEER_TB = 128
PEER_NBUF = 3


def _peer_apply_kernel(ex_ref, gt_ref, h_ref, x_ref, mod_ref, uv_ref, o_ref, buf_ref, h32_ref, sem):
    h32_ref[...] = h_ref[...].astype(F32)

    def issue(t, slot):
        for k in range(PEER_SEL):
            e = ex_ref[t, k]
            pltpu.make_async_copy(uv_ref.at[pl.ds(e, 1)], buf_ref.at[slot, pl.ds(k, 1)], sem.at[slot]).start()

    def wait(slot):
        pltpu.make_async_copy(uv_ref.at[pl.ds(0, PEER_SEL)], buf_ref.at[slot], sem.at[slot]).wait()

    for t0 in range(PEER_NBUF - 1):
        issue(t0, t0)

    lane = lax.broadcasted_iota(jnp.int32, (1, PEER_TB), 1)
    g2 = mod_ref[5:6, :]

    def body(t, carry):
        slot = t % PEER_NBUF
        nxt = t + (PEER_NBUF - 1)

        @pl.when(nxt < PEER_TB)
        def _():
            issue(nxt, nxt % PEER_NBUF)

        wait(slot)
        w = buf_ref[slot]
        u = pltpu.bitcast(w & jnp.uint32(0xFFFF0000), F32)
        v = pltpu.bitcast(w << 16, F32)
        hrow = h32_ref[pl.ds(t, 1), :]
        s = jnp.sum(u * hrow, axis=1, keepdims=True)
        gate = jnp.sum(jnp.where(lane == t, gt_ref[...], 0.0), axis=1, keepdims=True)
        act = _gelu(s) * gate
        out = jnp.sum(v * act, axis=0, keepdims=True)
        o_ref[pl.ds(t, 1), :] = x_ref[pl.ds(t, 1), :] + g2 * out
        return carry

    lax.fori_loop(0, PEER_TB, body, 0)


def _peer_apply(experts, gates_t, h2, x, mod, uv, *, seq):
    n, d = x.shape
    tb = PEER_TB
    return pl.pallas_call(
        _peer_apply_kernel,
        out_shape=jax.ShapeDtypeStruct((n, d), F32),
        grid=(n // tb,),
        in_specs=[pl.BlockSpec((tb, PEER_SEL), lambda i: (i, 0), memory_space=pltpu.SMEM),
                  pl.BlockSpec((PEER_SEL, tb), lambda i: (0, i)),
                  pl.BlockSpec((tb, d), lambda i: (i, 0)),
                  pl.BlockSpec((tb, d), lambda i: (i, 0)),
                  pl.BlockSpec((None, 6, d), lambda i: (i * tb // seq, 0, 0)),
                  pl.BlockSpec(memory_space=pl.ANY)],
        out_specs=pl.BlockSpec((tb, d), lambda i: (i, 0)),
        scratch_shapes=[pltpu.VMEM((PEER_NBUF, PEER_SEL, d), jnp.uint32),
                        pltpu.VMEM((tb, d), F32),
                        pltpu.SemaphoreType.DMA((PEER_NBUF,))],
        compiler_params=_cparams("arbitrary"),
        name="peer_apply",
    )(experts, gates_t, h2, x, mod, uv)


def _pack_uv(u_tab, v_tab):
    ub = lax.bitcast_convert_type(u_tab.astype(BF16), jnp.uint16).astype(jnp.uint32)
    vb = lax.bitcast_convert_type(v_tab.astype(BF16), jnp.uint16).astype(jnp.uint32)
    return (ub << 16) | vb


def _main_proj_weights(w_in):
    d = w_in.shape[0]
    main = jnp.concatenate([w_in[:, :4608], w_in[:, 4616:5448], w_in[:, 4608:4616],
                            jnp.zeros((d, P_WIDTH - 5448), w_in.dtype)], axis=1)
    return main.astype(BF16), w_in[:, 5448:].astype(BF16)


def _mla_weights(w_uq, w_ukv):
    qr = w_uq.shape[0]
    kvr = w_ukv.shape[0]
    wq = jnp.pad(w_uq.reshape(qr, N_HEADS, MLA_QK), ((0, 0), (0, 0), (0, MLA_PAD - MLA_QK))).reshape(qr, -1)
    kv = w_ukv.reshape(kvr, N_HEADS, MLA_NOPE + HEAD_W)
    k_nope = jnp.pad(kv[:, :, :MLA_NOPE], ((0, 0), (0, 0), (MLA_ROPE, MLA_PAD - MLA_QK))).reshape(kvr, -1)
    eye = jnp.pad(jnp.eye(MLA_ROPE, dtype=w_ukv.dtype), ((0, 128 - MLA_ROPE), (0, MLA_PAD - MLA_ROPE)))
    wk = jnp.concatenate([k_nope, jnp.tile(eye, (1, N_HEADS))], axis=0)
    wv = kv[:, :, MLA_NOPE:].reshape(kvr, -1)
    return wq.astype(BF16), wk.astype(BF16), wv.astype(BF16)


def kernel(x, c, positions, norm1_g, norm2_g, w_mod, b_mod, w_in, da_q_norm, da_k_norm, da_lambda, da_out_norm, sg_v_norm, sg_w, sg_b, ml_conv_w, ml_conv_b, ml_gate_b, ml_out_norm, mla_q_norm, mla_kv_norm, mla_w_uq, mla_w_ukv, mla_qk_norm, w_branch, w_out, peer_w_q, peer_sub_keys, peer_u, peer_v):
    batch, seq, d = x.shape
    depth = w_mod.shape[0]
    n = batch * seq
    xf = x.reshape(n, d)

    mod_all = _modulation(c, w_mod, b_mod)
    pos_col = positions.reshape(n, 1).astype(F32)
    cos_da, sin_da = _rope_tables(pos_col, _lane_freq_sign(DA_QK, DA_ROT))
    cos_ml, sin_ml = _rope_tables(pos_col, _lane_freq_sign(128, MLA_ROPE))

    for l in range(depth):
        mod = mod_all[l]
        lambda_init = 0.8 - 0.6 * math.exp(-0.3 * l)
        w_main, w_gates = _main_proj_weights(w_in[l])
        p = _norm_matmul(xf, norm1_g[l], mod, w_main, seq=seq, sc_row=1, sh_row=0, out_dtype=F32, name="in_proj")
        gates = _norm_matmul(xf, norm1_g[l], mod, w_gates, seq=seq, sc_row=1, sh_row=0, out_dtype=BF16,
                             name="gate_proj")

        qn, kn = _da_prep(p, cos_da, sin_da, da_q_norm[l], da_k_norm[l])
        ya = _da_attn(qn, kn, p, da_lambda[l], da_out_norm[l], batch=batch, seq=seq, lambda_init=lambda_init)

        yb = _spatial_gating(p, sg_v_norm[l], sg_w[l], sg_b[l])

        qk = _ml_conv(p, ml_conv_w[l], ml_conv_b[l], seq=seq)
        yc = _mlstm(qk, p, ml_gate_b[l], ml_out_norm[l], batch=batch, seq=seq)

        wq, wk, wv = _mla_weights(mla_w_uq[l], mla_w_ukv[l])
        qk_g = jnp.pad(mla_qk_norm[l], ((0, 0), (0, MLA_PAD - MLA_QK)))
        qm, km, vm = _mla_prep(p, cos_ml, sin_ml, mla_q_norm[l], mla_kv_norm[l], wq, wk, wv, qk_g)
        yd = _mla_attn(qm, km, vm, batch=batch, seq=seq)

        mixed = _merge((ya, yb, yc, yd), gates, w_branch[l].astype(BF16))
        xf = _resid_matmul(mixed, w_out[l].astype(BF16), xf, mod, seq=seq, gate_row=2)

        q_peer, h2 = _norm_matmul(xf, norm2_g[l], mod, peer_w_q[l].astype(BF16), seq=seq, sc_row=4, sh_row=3,
                                  out_dtype=F32, emit_h=True, name="peer_query")
        experts_t, gates_t = _peer_route(q_peer, peer_sub_keys[l])
        uv = _pack_uv(peer_u[l], peer_v[l])
        xf = _peer_apply(experts_t.T, gates_t, h2, xf, mod, uv, seq=seq)

    return xf.reshape(batch, seq, d)
```

```python
import functools
import math

import numpy as np
import jax
import jax.numpy as jnp
from jax import lax
from jax.experimental import pallas as pl
from jax.experimental.pallas import tpu as pltpu

F32 = jnp.float32
BF16 = jnp.bfloat16

D_MODEL = 2048
CHUNK = 64
ROPE_THETA = 500000.0
RMS_EPS = 1e-6
N_HEADS = 4
HEAD_W = 128
BRANCH_W = N_HEADS * HEAD_W
DA_QK = 64
DA_ROT = 16
ML_CONV = 4
MLA_ROPE = 64
MLA_NOPE = 128
MLA_QK = MLA_ROPE + MLA_NOPE
MLA_PAD = 256
MLA_Q_RANK = 512
MLA_KV_RANK = 256
PEER_HEADS = 8
PEER_KEYS = 128
PEER_TOPK = 16
PEER_SEL = PEER_HEADS * PEER_TOPK

P_WIDTH = 11 * 512
COL_DAQ, COL_DAK, COL_DAV, COL_SGU, COL_SGV, COL_MLQK, COL_MLV, COL_MLO, COL_CQ, COL_MISC = 0, 1, 2, 3, 4, 5, 7, 8, 9, 10
MISC_KR = 256
MISC_GATE = 320

VMEM_LIMIT = 56 * 1024 * 1024
NEG_INF = float("-inf")
LOG2_E = math.log2(math.e)


def _cparams(*sem):
    return pltpu.CompilerParams(dimension_semantics=sem, vmem_limit_bytes=VMEM_LIMIT)


def _gelu(x):
    return 0.5 * x * (1.0 + lax.erf(x * (1.0 / math.sqrt(2.0))))


def _dot_nt(a, b):
    return lax.dot_general(a, b, (((1,), (1,)), ((), ())), preferred_element_type=F32)


def _mod_kernel(c_ref, w_ref, b_ref, o_ref):
    c = c_ref[...]
    cond = c * jax.nn.sigmoid(c)
    o_ref[0] = jnp.dot(cond, w_ref[0], preferred_element_type=F32,
                       precision=lax.Precision.HIGHEST) + b_ref[0]


def _modulation(c, w_mod, b_mod):
    depth, d, w6 = w_mod.shape
    b = c.shape[0]
    rows = 8
    cp = jnp.zeros((rows, d), F32).at[:b].set(c)
    tn = 1024
    out = pl.pallas_call(
        _mod_kernel,
        out_shape=jax.ShapeDtypeStruct((depth, rows, w6), F32),
        grid=(depth, w6 // tn),
        in_specs=[pl.BlockSpec((rows, d), lambda l, j: (0, 0)),
                  pl.BlockSpec((1, d, tn), lambda l, j: (l, 0, j)),
                  pl.BlockSpec((1, 1, tn), lambda l, j: (l, 0, j))],
        out_specs=pl.BlockSpec((1, rows, tn), lambda l, j: (l, 0, j)),
        compiler_params=_cparams("parallel", "parallel"),
        name="modulation",
    )(cp, w_mod, b_mod.reshape(depth, 1, w6))
    return out[:, :b].reshape(depth, b, 6, d)


def _nm_kernel(x_ref, g_ref, mod_ref, w_ref, *rest, sc_row, sh_row, emit_h):
    if emit_h:
        o_ref, h_out_ref, h_ref = rest
    else:
        o_ref, h_ref = rest

    @pl.when(pl.program_id(1) == 0)
    def _():
        x = x_ref[...]
        ms = jnp.mean(x * x, axis=-1, keepdims=True)
        y = x * lax.rsqrt(ms + RMS_EPS) * g_ref[...]
        h = y * (1.0 + mod_ref[sc_row:sc_row + 1, :]) + mod_ref[sh_row:sh_row + 1, :]
        h_ref[...] = h.astype(BF16)
        if emit_h:
            h_out_ref[...] = h.astype(BF16)

    o_ref[...] = jnp.dot(h_ref[...], w_ref[...], preferred_element_type=F32).astype(o_ref.dtype)


def _norm_matmul(x, g, mod, w, *, seq, sc_row, sh_row, out_dtype, emit_h=False, tm=512, tn=512, name):
    n, d = x.shape
    wn = w.shape[1]
    out_shape = [jax.ShapeDtypeStruct((n, wn), out_dtype)]
    out_specs = [pl.BlockSpec((tm, tn), lambda i, j: (i, j))]
    if emit_h:
        out_shape.append(jax.ShapeDtypeStruct((n, d), BF16))
        out_specs.append(pl.BlockSpec((tm, d), lambda i, j: (i, 0)))
    res = pl.pallas_call(
        functools.partial(_nm_kernel, sc_row=sc_row, sh_row=sh_row, emit_h=emit_h),
        out_shape=out_shape,
        grid=(n // tm, wn // tn),
        in_specs=[pl.BlockSpec((tm, d), lambda i, j: (i, 0)),
                  pl.BlockSpec((1, d), lambda i, j: (0, 0)),
                  pl.BlockSpec((None, 6, d), lambda i, j: (i * tm // seq, 0, 0)),
                  pl.BlockSpec((d, tn), lambda i, j: (0, j))],
        out_specs=out_specs,
        scratch_shapes=[pltpu.VMEM((tm, d), BF16)],
        compiler_params=_cparams("parallel", "arbitrary"),
        name=name,
    )(x, g.reshape(1, d), mod, w)
    return res if emit_h else res[0]


def _rope_table_kernel(pos_ref, f_ref, cos_ref, sin_ref):
    ang = pos_ref[...] * f_ref[0:1, :]
    cos_ref[...] = jnp.cos(ang)
    sin_ref[...] = jnp.sin(ang) * f_ref[1:2, :]


def _rope_tables(pos_col, freq_sign, tm=1024):
    n = pos_col.shape[0]
    return pl.pallas_call(
        _rope_table_kernel,
        out_shape=[jax.ShapeDtypeStruct((n, 128), F32)] * 2,
        grid=(n // tm,),
        in_specs=[pl.BlockSpec((tm, 1), lambda i: (i, 0)),
                  pl.BlockSpec((2, 128), lambda i: (0, 0))],
        out_specs=[pl.BlockSpec((tm, 128), lambda i: (i, 0))] * 2,
        compiler_params=_cparams("parallel"),
        name="rope_tables",
    )(pos_col, freq_sign)


def _lane_freq_sign(group, rot):
    half = rot // 2
    freq = ROPE_THETA ** (-jnp.arange(half, dtype=F32) / half)
    lane = np.arange(128) % group
    idx = np.where(lane < rot, lane % half, 0)
    f = jnp.where(jnp.asarray(lane < rot), freq[idx], 0.0)
    sign = np.where(lane < half, -1.0, np.where(lane < rot, 1.0, 0.0)).astype(np.float32)
    return jnp.stack([f, jnp.asarray(sign)], axis=0)


def _rope_apply(x, cos, sin, group, rot):
    half = rot // 2
    lane = lax.broadcasted_iota(jnp.int32, (1, 128), 1) % group
    partner = jnp.where(lane < half, pltpu.roll(x, 128 - half, 1), pltpu.roll(x, half, 1))
    return x * cos + partner * sin


def _da_prep_kernel(q_ref, k_ref, cos_ref, sin_ref, qg_ref, kg_ref, qo_ref, ko_ref):
    cos = cos_ref[...]
    sin = sin_ref[...]
    lo = lax.broadcasted_iota(jnp.int32, (1, 128), 1) < DA_QK

    def prep(ref, g_ref, o_ref, scale):
        for h in range(N_HEADS):
            x = ref[:, h * HEAD_W:(h + 1) * HEAD_W]
            sq = x * x
            s_lo = jnp.sum(jnp.where(lo, sq, 0.0), axis=-1, keepdims=True)
            s_hi = jnp.sum(jnp.where(lo, 0.0, sq), axis=-1, keepdims=True)
            ms = jnp.where(lo, s_lo, s_hi) * (1.0 / DA_QK)
            y = x * lax.rsqrt(ms + RMS_EPS) * g_ref[...]
            y = _rope_apply(y, cos, sin, DA_QK, DA_ROT)
            o_ref[:, h * HEAD_W:(h + 1) * HEAD_W] = (y * scale).astype(BF16)

    prep(q_ref, qg_ref, qo_ref, DA_QK ** -0.5 * LOG2_E)
    prep(k_ref, kg_ref, ko_ref, 1.0)


def _da_prep(p, cos, sin, q_g, k_g, tm=512):
    n = p.shape[0]
    return pl.pallas_call(
        _da_prep_kernel,
        out_shape=[jax.ShapeDtypeStruct((n, BRANCH_W), BF16)] * 2,
        grid=(n // tm,),
        in_specs=[pl.BlockSpec((tm, BRANCH_W), lambda i: (i, COL_DAQ)),
                  pl.BlockSpec((tm, BRANCH_W), lambda i: (i, COL_DAK)),
                  pl.BlockSpec((tm, 128), lambda i: (i, 0)),
                  pl.BlockSpec((tm, 128), lambda i: (i, 0)),
                  pl.BlockSpec((1, 128), lambda i: (0, 0)),
                  pl.BlockSpec((1, 128), lambda i: (0, 0))],
        out_specs=[pl.BlockSpec((tm, BRANCH_W), lambda i: (i, 0))] * 2,
        compiler_params=_cparams("parallel"),
        name="da_prep",
    )(p, p, cos, sin, q_g.reshape(1, 128), k_g.reshape(1, 128))


def _causal_pairs(nq, tq, tk):
    qi, ki, last = [], [], []
    for q in range(nq):
        nk = -(-((q + 1) * tq) // tk)
        for k in range(nk):
            qi.append(q)
            ki.append(k)
            last.append(int(k == nk - 1))
    return (jnp.asarray(qi, jnp.int32), jnp.asarray(ki, jnp.int32), jnp.asarray(last, jnp.int32))


def _softmax_step(s, v, idx, m_ref, l_ref, acc_ref):
    m_prev = m_ref[idx]
    m_new = jnp.maximum(m_prev, jnp.max(s, axis=0, keepdims=True))
    alpha = jnp.exp2(m_prev - m_new)
    p = jnp.exp2(s - m_new)
    l_ref[idx] = alpha * l_ref[idx] + jnp.sum(p, axis=0, keepdims=True)
    pv = lax.dot_general(v, p.astype(BF16), (((0,), (0,)), ((), ())), preferred_element_type=F32)
    acc_ref[idx] = alpha * acc_ref[idx] + pv
    m_ref[idx] = m_new


def _chunk_mask_t(qi, ki, tq, tk):
    key = ki * tk + lax.broadcasted_iota(jnp.int32, (tk, 1), 0)
    qry = qi * tq + lax.broadcasted_iota(jnp.int32, (1, tq), 1)
    return (key // CHUNK) <= (qry // CHUNK)


def _init_softmax_state(m_ref, l_ref, acc_ref):
    m_ref[...] = jnp.full(m_ref.shape, NEG_INF, F32)
    l_ref[...] = jnp.zeros(l_ref.shape, F32)
    acc_ref[...] = jnp.zeros(acc_ref.shape, F32)


def _da_attn_kernel(qi_ref, ki_ref, last_ref, q_ref, k_ref, v_ref, lam_ref, og_ref, o_ref,
                    m_ref, l_ref, acc_ref, *, lambda_init, tq, tk):
    p_id = pl.program_id(1)
    qi = qi_ref[p_id]
    ki = ki_ref[p_id]

    @pl.when(ki == 0)
    def _():
        _init_softmax_state(m_ref, l_ref, acc_ref)

    lo = lax.broadcasted_iota(jnp.int32, (1, HEAD_W), 1) < DA_QK

    def attend(mask):
        for h in range(N_HEADS):
            sl = slice(h * HEAD_W, (h + 1) * HEAD_W)
            q = q_ref[:, sl]
            k = k_ref[:, sl]
            v = v_ref[:, sl].astype(BF16)
            for c in range(2):
                qc = jnp.where(lo, q, jnp.zeros_like(q)) if c == 0 else jnp.where(lo, jnp.zeros_like(q), q)
                s = _dot_nt(k, qc)
                if mask is not None:
                    s = jnp.where(mask, s, NEG_INF)
                _softmax_step(s, v, 2 * h + c, m_ref, l_ref, acc_ref)

    @pl.when(ki == qi)
    def _():
        attend(_chunk_mask_t(qi, ki, tq, tk))

    @pl.when(ki != qi)
    def _():
        attend(None)

    @pl.when(last_ref[p_id] == 1)
    def _():
        lf = lam_ref[...]
        lam = (jnp.exp(jnp.sum(lf[0:1] * lf[1:2], axis=-1, keepdims=True))
               - jnp.exp(jnp.sum(lf[2:3] * lf[3:4], axis=-1, keepdims=True)) + lambda_init)
        for h in range(N_HEADS):
            o = acc_ref[2 * h] / l_ref[2 * h] - lam * (acc_ref[2 * h + 1] / l_ref[2 * h + 1])
            ms = jnp.mean(o * o, axis=0, keepdims=True)
            y = o * lax.rsqrt(ms + RMS_EPS) * og_ref[...] * (1.0 - lambda_init)
            o_ref[:, h * HEAD_W:(h + 1) * HEAD_W] = y.T.astype(o_ref.dtype)


def _da_attn(qn, kn, p, lam_vecs, out_g, *, batch, seq, lambda_init, tq=512, tk=512):
    assert tq == tk
    n = qn.shape[0]
    nq, nk = seq // tq, seq // tk
    qi, ki, last = _causal_pairs(nq, tq, tk)
    grid_spec = pltpu.PrefetchScalarGridSpec(
        num_scalar_prefetch=3,
        grid=(batch, int(qi.shape[0])),
        in_specs=[pl.BlockSpec((tq, BRANCH_W), lambda b, p_, qi_, ki_, la_: (b * nq + qi_[p_], 0)),
                  pl.BlockSpec((tk, BRANCH_W), lambda b, p_, qi_, ki_, la_: (b * nk + ki_[p_], 0)),
                  pl.BlockSpec((tk, BRANCH_W), lambda b, p_, qi_, ki_, la_: (b * nk + ki_[p_], COL_DAV)),
                  pl.BlockSpec((4, DA_QK), lambda b, p_, qi_, ki_, la_: (0, 0)),
                  pl.BlockSpec((HEAD_W, 1), lambda b, p_, qi_, ki_, la_: (0, 0))],
        out_specs=pl.BlockSpec((tq, BRANCH_W), lambda b, p_, qi_, ki_, la_: (b * nq + qi_[p_], 0)),
        scratch_shapes=[pltpu.VMEM((2 * N_HEADS, 1, tq), F32),
                        pltpu.VMEM((2 * N_HEADS, 1, tq), F32),
                        pltpu.VMEM((2 * N_HEADS, HEAD_W, tq), F32)],
    )
    return pl.pallas_call(
        functools.partial(_da_attn_kernel, lambda_init=lambda_init, tq=tq, tk=tk),
        out_shape=jax.ShapeDtypeStruct((n, BRANCH_W), BF16),
        grid_spec=grid_spec,
        compiler_params=_cparams("parallel", "arbitrary"),
        name="da_attn",
    )(qi, ki, last, qn, kn, p, lam_vecs, out_g.reshape(HEAD_W, 1))


def _sg_kernel(u_ref, v_ref, vg_ref, w_ref, b_ref, o_ref, *, tm):
    idx_i = lax.broadcasted_iota(jnp.int32, (128, 128), 0)
    idx_j = lax.broadcasted_iota(jnp.int32, (128, 128), 1)
    wmask = (idx_j // CHUNK) <= (idx_i // CHUNK)
    for g in range(N_HEADS):
        sl = slice(g * HEAD_W, (g + 1) * HEAD_W)
        u = _gelu(u_ref[:, sl])
        v = _gelu(v_ref[:, sl])
        ms = jnp.mean(v * v, axis=-1, keepdims=True)
        vn = (v * lax.rsqrt(ms + RMS_EPS) * vg_ref[g:g + 1, :]).astype(BF16)
        w = jnp.where(wmask, w_ref[g], 0.0).astype(BF16)
        bias = b_ref[:, g:g + 1]
        for nb in range(tm // 128):
            rows = slice(nb * 128, (nb + 1) * 128)
            sv = jnp.dot(w, vn[rows], preferred_element_type=F32) + bias
            o_ref[rows, sl] = (u[rows] * sv).astype(o_ref.dtype)


def _spatial_gating(p, v_g, w_s, b_s, tm=512):
    n = p.shape[0]
    return pl.pallas_call(
        functools.partial(_sg_kernel, tm=tm),
        out_shape=jax.ShapeDtypeStruct((n, BRANCH_W), BF16),
        grid=(n // tm,),
        in_specs=[pl.BlockSpec((tm, BRANCH_W), lambda i: (i, COL_SGU)),
                  pl.BlockSpec((tm, BRANCH_W), lambda i: (i, COL_SGV)),
                  pl.BlockSpec((N_HEADS, 128), lambda i: (0, 0)),
                  pl.BlockSpec((N_HEADS, 128, 128), lambda i: (0, 0, 0)),
                  pl.BlockSpec((128, N_HEADS), lambda i: (0, 0))],
        out_specs=pl.BlockSpec((tm, BRANCH_W), lambda i: (i, 0)),
        compiler_params=_cparams("parallel"),
        name="spatial_gating",
    )(p, p, v_g, w_s, b_s.T)


def _ml_conv_kernel(x_ref, halo_ref, w_ref, b_ref, o_ref, *, tm, seq):
    i = pl.program_id(0)
    first = (i * tm) % seq == 0
    halo = jnp.where(first, 0.0, halo_ref[...])
    xc = jnp.concatenate([halo, x_ref[...]], axis=0)
    acc = jnp.zeros((tm, x_ref.shape[1]), F32) + b_ref[...]
    for j in range(ML_CONV):
        shift = ML_CONV - 1 - j
        xs = xc if shift == 0 else pltpu.roll(xc, shift, 0)
        acc = acc + xs[8:8 + tm] * w_ref[j:j + 1, :]
    o_ref[...] = acc * jax.nn.sigmoid(acc)


def _ml_conv(p, conv_w, conv_b, *, seq, tm=512):
    n = p.shape[0]
    width = 2 * BRANCH_W
    cb = width // BRANCH_W
    return pl.pallas_call(
        functools.partial(_ml_conv_kernel, tm=tm, seq=seq),
        out_shape=jax.ShapeDtypeStruct((n, width), F32),
        grid=(n // tm, cb),
        in_specs=[pl.BlockSpec((tm, BRANCH_W), lambda i, j: (i, COL_MLQK + j)),
                  pl.BlockSpec((8, BRANCH_W), lambda i, j: (jnp.maximum(i * (tm // 8) - 1, 0), COL_MLQK + j)),
                  pl.BlockSpec((ML_CONV, BRANCH_W), lambda i, j: (0, j)),
                  pl.BlockSpec((1, BRANCH_W), lambda i, j: (0, j))],
        out_specs=pl.BlockSpec((tm, BRANCH_W), lambda i, j: (i, j)),
        compiler_params=_cparams("parallel", "parallel"),
        name="ml_conv",
    )(p, p, conv_w, conv_b.reshape(1, width))


def _log_sigmoid(x):
    return jnp.minimum(x, 0.0) - jnp.log1p(jnp.exp(-jnp.abs(x)))


def _mlstm_kernel(q_ref, k_ref, v_ref, o_ref, gc_ref, gr_ref, bc_ref, br_ref, og_ref, y_ref,
                  c_ref, n_ref, m_ref):
    @pl.when(pl.program_id(1) == 0)
    def _():
        c_ref[...] = jnp.zeros(c_ref.shape, F32)
        n_ref[...] = jnp.zeros(n_ref.shape, F32)
        m_ref[...] = jnp.zeros(m_ref.shape, F32)

    L = CHUNK
    it = lax.broadcasted_iota(jnp.int32, (L, L), 0)
    is_ = lax.broadcasted_iota(jnp.int32, (L, L), 1)
    tril = is_ <= it
    gcol = gc_ref[:, MISC_GATE:MISC_GATE + 2 * N_HEADS] + bc_ref[...]
    grow = gr_ref[...] + br_ref[...]
    for h in range(N_HEADS):
        sl = slice(h * HEAD_W, (h + 1) * HEAD_W)
        q = q_ref[:, sl].astype(BF16)
        k = (k_ref[:, sl] * (HEAD_W ** -0.5))
        kb = k.astype(BF16)
        v = v_ref[:, sl]
        vb = v.astype(BF16)
        li_row = grow[h:h + 1, :]
        lf_row = _log_sigmoid(grow[N_HEADS + h:N_HEADS + h + 1, :])
        li_col = gcol[:, h:h + 1]
        lf_col = _log_sigmoid(gcol[:, N_HEADS + h:N_HEADS + h + 1])
        bcum_col = jnp.sum(jnp.where(tril, lf_row, 0.0), axis=1, keepdims=True)
        bcum_row = jnp.sum(jnp.where(tril, 0.0, lf_col) + jnp.where(is_ == it, lf_col, 0.0),
                           axis=0, keepdims=True)
        g = jnp.sum(lf_row, axis=1, keepdims=True)
        m_prev = m_ref[h][:, 0:1]
        dmat = jnp.where(tril, bcum_col - bcum_row + li_row, NEG_INF)
        m_inter = bcum_col + m_prev
        m_t = jnp.maximum(m_inter, jnp.max(dmat, axis=1, keepdims=True))
        a_inter = jnp.exp(m_inter - m_t)
        sm = jnp.exp(dmat - m_t) * _dot_nt(q, kb)
        num = (a_inter * jnp.dot(q, c_ref[h].astype(BF16), preferred_element_type=F32)
               + jnp.dot(sm.astype(BF16), vb, preferred_element_type=F32))
        qn = jnp.sum(q.astype(F32) * n_ref[h], axis=1, keepdims=True)
        den = a_inter * qn + jnp.sum(sm, axis=1, keepdims=True)
        h_t = num / jnp.maximum(jnp.abs(den), jnp.exp(-m_t))
        dec_row = g - bcum_row + li_row
        dec_col = g - bcum_col + li_col
        m_new = jnp.maximum(g + m_prev, jnp.max(dec_row, axis=1, keepdims=True))
        w_col = jnp.exp(dec_col - m_new)
        a = jnp.exp(g + m_prev - m_new)
        kw = w_col * k
        c_ref[h] = a * c_ref[h] + lax.dot_general(kw.astype(BF16), vb, (((0,), (0,)), ((), ())),
                                                  preferred_element_type=F32)
        n_ref[h] = a * n_ref[h] + jnp.sum(kw, axis=0, keepdims=True)
        m_ref[h] = jnp.broadcast_to(m_new, (1, HEAD_W))
        y = jax.nn.sigmoid(o_ref[:, sl]) * h_t
        ms = jnp.mean(y * y, axis=-1, keepdims=True)
        y_ref[:, sl] = (y * lax.rsqrt(ms + RMS_EPS) * og_ref[h:h + 1, :]).astype(y_ref.dtype)


def _mlstm(qk, p, gate_b, out_g, *, batch, seq):
    n = p.shape[0]
    nc = seq // CHUNK
    g8 = p[:, COL_MISC * 512 + MISC_GATE:COL_MISC * 512 + MISC_GATE + 2 * N_HEADS]
    grow = jnp.swapaxes(g8.reshape(batch, nc, CHUNK, 2 * N_HEADS), 2, 3)
    bias = gate_b.reshape(1, 2 * N_HEADS)
    tok = lambda b, c: (b * nc + c, 0)
    return pl.pallas_call(
        _mlstm_kernel,
        out_shape=jax.ShapeDtypeStruct((n, BRANCH_W), BF16),
        grid=(batch, nc),
        in_specs=[pl.BlockSpec((CHUNK, BRANCH_W), lambda b, c: (b * nc + c, 0)),
                  pl.BlockSpec((CHUNK, BRANCH_W), lambda b, c: (b * nc + c, 1)),
                  pl.BlockSpec((CHUNK, BRANCH_W), lambda b, c: (b * nc + c, COL_MLV)),
                  pl.BlockSpec((CHUNK, BRANCH_W), lambda b, c: (b * nc + c, COL_MLO)),
                  pl.BlockSpec((CHUNK, BRANCH_W), lambda b, c: (b * nc + c, COL_MISC)),
                  pl.BlockSpec((None, None, 2 * N_HEADS, CHUNK), lambda b, c: (b, c, 0, 0)),
                  pl.BlockSpec((1, 2 * N_HEADS), lambda b, c: (0, 0)),
                  pl.BlockSpec((2 * N_HEADS, 1), lambda b, c: (0, 0)),
                  pl.BlockSpec((N_HEADS, HEAD_W), lambda b, c: (0, 0))],
        out_specs=pl.BlockSpec((CHUNK, BRANCH_W), tok),
        scratch_shapes=[pltpu.VMEM((N_HEADS, HEAD_W, HEAD_W), F32),
                        pltpu.VMEM((N_HEADS, 1, HEAD_W), F32),
                        pltpu.VMEM((N_HEADS, 1, HEAD_W), F32)],
        compiler_params=_cparams("parallel", "arbitrary"),
        name="mlstm",
    )(qk, qk, p, p, p, grow, bias, bias.reshape(2 * N_HEADS, 1), out_g)


def _mla_prep_kernel(cq_ref, misc_ref, cos_ref, sin_ref, qa_ref, kva_ref, wq_ref, wk_ref, wv_ref, qkg_ref,
                     qo_ref, ko_ref, vo_ref):
    def rms(x, g):
        ms = jnp.mean(x * x, axis=-1, keepdims=True)
        return x * lax.rsqrt(ms + RMS_EPS) * g

    cqn = rms(cq_ref[...], qa_ref[...]).astype(BF16)
    ckvn = rms(misc_ref[:, :MLA_KV_RANK], kva_ref[...]).astype(BF16)
    lane = lax.broadcasted_iota(jnp.int32, (1, 128), 1)
    kr = jnp.where(lane < MLA_ROPE, misc_ref[:, MISC_KR:MISC_KR + 128], 0.0).astype(BF16)
    q_lin = jnp.dot(cqn, wq_ref[...], preferred_element_type=F32)
    k_lin = jnp.dot(jnp.concatenate([ckvn, kr], axis=1), wk_ref[...], preferred_element_type=F32)
    vo_ref[...] = jnp.dot(ckvn, wv_ref[...], preferred_element_type=F32).astype(vo_ref.dtype)
    cos = cos_ref[...]
    sin = sin_ref[...]

    def finish(x_lin, g_row, o_ref, scale):
        for h in range(N_HEADS):
            x = x_lin[:, h * MLA_PAD:(h + 1) * MLA_PAD]
            ms = jnp.sum(x * x, axis=-1, keepdims=True) * (1.0 / MLA_QK)
            y = x * lax.rsqrt(ms + RMS_EPS) * qkg_ref[g_row:g_row + 1, :]
            y0 = _rope_apply(y[:, :128], cos, sin, 128, MLA_ROPE)
            o_ref[:, h * MLA_PAD:h * MLA_PAD + 128] = (y0 * scale).astype(o_ref.dtype)
            o_ref[:, h * MLA_PAD + 128:(h + 1) * MLA_PAD] = (y[:, 128:] * scale).astype(o_ref.dtype)

    finish(q_lin, 0, qo_ref, MLA_QK ** -0.5 * LOG2_E)
    finish(k_lin, 1, ko_ref, 1.0)


def _mla_prep(p, cos, sin, q_a_g, kv_a_g, wq, wk, wv, qk_g, tm=512):
    n = p.shape[0]
    hw = N_HEADS * MLA_PAD
    full = lambda shape: pl.BlockSpec(shape, lambda i: (0,) * len(shape))
    return pl.pallas_call(
        _mla_prep_kernel,
        out_shape=[jax.ShapeDtypeStruct((n, hw), BF16), jax.ShapeDtypeStruct((n, hw), BF16),
                   jax.ShapeDtypeStruct((n, BRANCH_W), BF16)],
        grid=(n // tm,),
        in_specs=[pl.BlockSpec((tm, 512), lambda i: (i, COL_CQ)),
                  pl.BlockSpec((tm, 512), lambda i: (i, COL_MISC)),
                  pl.BlockSpec((tm, 128), lambda i: (i, 0)),
                  pl.BlockSpec((tm, 128), lambda i: (i, 0)),
                  full((1, MLA_Q_RANK)), full((1, MLA_KV_RANK)),
                  full(wq.shape), full(wk.shape), full(wv.shape), full((2, MLA_PAD))],
        out_specs=[pl.BlockSpec((tm, hw), lambda i: (i, 0)), pl.BlockSpec((tm, hw), lambda i: (i, 0)),
                   pl.BlockSpec((tm, BRANCH_W), lambda i: (i, 0))],
        compiler_params=_cparams("parallel"),
        name="mla_prep",
    )(p, p, cos, sin, q_a_g.reshape(1, -1), kv_a_g.reshape(1, -1), wq, wk, wv, qk_g)


def _mla_attn_kernel(qi_ref, ki_ref, last_ref, q_ref, k_ref, v_ref, o_ref, m_ref, l_ref, acc_ref, *, tq, tk):
    p_id = pl.program_id(1)
    qi = qi_ref[p_id]
    ki = ki_ref[p_id]

    @pl.when(ki == 0)
    def _():
        _init_softmax_state(m_ref, l_ref, acc_ref)

    def attend(mask):
        for h in range(N_HEADS):
            q = q_ref[:, h * MLA_PAD:(h + 1) * MLA_PAD]
            k = k_ref[:, h * MLA_PAD:(h + 1) * MLA_PAD]
            v = v_ref[:, h * HEAD_W:(h + 1) * HEAD_W]
            s = _dot_nt(k, q)
            if mask is not None:
                s = jnp.where(mask, s, NEG_INF)
            _softmax_step(s, v, h, m_ref, l_ref, acc_ref)

    @pl.when(ki == qi)
    def _():
        attend(_chunk_mask_t(qi, ki, tq, tk))

    @pl.when(ki != qi)
    def _():
        attend(None)

    @pl.when(last_ref[p_id] == 1)
    def _():
        for h in range(N_HEADS):
            o_ref[:, h * HEAD_W:(h + 1) * HEAD_W] = (acc_ref[h] / l_ref[h]).T.astype(o_ref.dtype)


def _mla_attn(qm, km, vm, *, batch, seq, tq=512, tk=512):
    assert tq == tk
    n = qm.shape[0]
    hw = N_HEADS * MLA_PAD
    nq, nk = seq // tq, seq // tk
    qi, ki, last = _causal_pairs(nq, tq, tk)
    grid_spec = pltpu.PrefetchScalarGridSpec(
        num_scalar_prefetch=3,
        grid=(batch, int(qi.shape[0])),
        in_specs=[pl.BlockSpec((tq, hw), lambda b, p_, qi_, ki_, la_: (b * nq + qi_[p_], 0)),
                  pl.BlockSpec((tk, hw), lambda b, p_, qi_, ki_, la_: (b * nk + ki_[p_], 0)),
                  pl.BlockSpec((tk, BRANCH_W), lambda b, p_, qi_, ki_, la_: (b * nk + ki_[p_], 0))],
        out_specs=pl.BlockSpec((tq, BRANCH_W), lambda b, p_, qi_, ki_, la_: (b * nq + qi_[p_], 0)),
        scratch_shapes=[pltpu.VMEM((N_HEADS, 1, tq), F32),
                        pltpu.VMEM((N_HEADS, 1, tq), F32),
                        pltpu.VMEM((N_HEADS, HEAD_W, tq), F32)],
    )
    return pl.pallas_call(
        functools.partial(_mla_attn_kernel, tq=tq, tk=tk),
        out_shape=jax.ShapeDtypeStruct((n, BRANCH_W), BF16),
        grid_spec=grid_spec,
        compiler_params=_cparams("parallel", "arbitrary"),
        name="mla_attn",
    )(qi, ki, last, qm, km, vm)


def _merge_kernel(ya_ref, yb_ref, yc_ref, yd_ref, g0_ref, g1_ref, g2_ref, g3_ref, wb_ref, o_ref):
    acc = None
    for n, (y_ref, g_ref) in enumerate(((ya_ref, g0_ref), (yb_ref, g1_ref), (yc_ref, g2_ref), (yd_ref, g3_ref))):
        up = jnp.dot(y_ref[...], wb_ref[n], preferred_element_type=F32)
        term = jax.nn.sigmoid(g_ref[...].astype(F32)) * up
        acc = term if acc is None else acc + term
    o_ref[...] = acc.astype(o_ref.dtype)


def _merge(ys, gates, wb, tm=1024, tn=512):
    n = gates.shape[0]
    nj = D_MODEL // tn
    y_spec = pl.BlockSpec((tm, BRANCH_W), lambda i, j: (i, 0))
    g_specs = [pl.BlockSpec((tm, tn), functools.partial(lambda i, j, nn: (i, nn * nj + j), nn=nn)) for nn in range(4)]
    return pl.pallas_call(
        _merge_kernel,
        out_shape=jax.ShapeDtypeStruct((n, D_MODEL), BF16),
        grid=(n // tm, nj),
        in_specs=[y_spec] * 4 + g_specs + [pl.BlockSpec((4, BRANCH_W, tn), lambda i, j: (0, 0, j))],
        out_specs=pl.BlockSpec((tm, tn), lambda i, j: (i, j)),
        compiler_params=_cparams("parallel", "arbitrary"),
        name="merge",
    )(*ys, gates, gates, gates, gates, wb)


def _resid_matmul_kernel(m_ref, w_ref, x_ref, mod_ref, o_ref, *, gate_row):
    y = jnp.dot(m_ref[...], w_ref[...], preferred_element_type=F32)
    o_ref[...] = x_ref[...] + mod_ref[gate_row:gate_row + 1, :] * y


def _resid_matmul(m, w, x, mod, *, seq, gate_row, tm=1024, tn=512):
    n, d = x.shape
    kdim = m.shape[1]
    return pl.pallas_call(
        functools.partial(_resid_matmul_kernel, gate_row=gate_row),
        out_shape=jax.ShapeDtypeStruct((n, d), F32),
        grid=(n // tm, d // tn),
        in_specs=[pl.BlockSpec((tm, kdim), lambda i, j: (i, 0)),
                  pl.BlockSpec((kdim, tn), lambda i, j: (0, j)),
                  pl.BlockSpec((tm, tn), lambda i, j: (i, j)),
                  pl.BlockSpec((None, 6, tn), lambda i, j: (i * tm // seq, 0, j))],
        out_specs=pl.BlockSpec((tm, tn), lambda i, j: (i, j)),
        compiler_params=_cparams("parallel", "arbitrary"),
        name="out_proj",
    )(m, w, x, mod)


def _top1_rounds(s, ids, rounds):
    nrow = s.shape[0]
    row = lax.broadcasted_iota(jnp.int32, s.shape, 0).astype(F32)
    vals, outs = [], []
    for _ in range(rounds):
        m = jnp.max(s, axis=0, keepdims=True)
        pos = jnp.min(jnp.where(s == m, row, float(nrow)), axis=0, keepdims=True)
        sel = row == pos
        vals.append(m)
        outs.append(pos if ids is None else jnp.sum(jnp.where(sel, ids, 0), axis=0, keepdims=True))
        s = jnp.where(sel, NEG_INF, s)
    out = jnp.concatenate(outs, axis=0)
    return jnp.concatenate(vals, axis=0), (out.astype(jnp.int32) if ids is None else out)


_CAND_ROWS = [(a, PEER_TOPK // (a + 1)) for a in range(PEER_TOPK)]
_CAND_PAD = -sum(nb for _, nb in _CAND_ROWS) % 8


def _route_kernel(q_ref, keys_ref, e_ref, g_ref, *, tt):
    for sub in range(tt // 128):
        rows = slice(sub * 128, (sub + 1) * 128)
        tops = []
        for c in range(2):
            qc = q_ref[rows, c * 128:(c + 1) * 128]
            sc = lax.dot_general(keys_ref[c], qc, (((1,), (1,)), ((), ())), preferred_element_type=F32,
                                 precision=lax.Precision.HIGHEST)
            tops.append(_top1_rounds(sc, None, PEER_TOPK))
        (s0, i0), (s1, i1) = tops
        pad_s = [jnp.full((_CAND_PAD, 128), NEG_INF, F32)] if _CAND_PAD else []
        pad_i = [jnp.zeros((_CAND_PAD, 128), jnp.int32)] if _CAND_PAD else []
        cand_s = jnp.concatenate([s0[a:a + 1] + s1[:nb] for a, nb in _CAND_ROWS] + pad_s, axis=0)
        cand_i = jnp.concatenate([i0[a:a + 1] * PEER_KEYS + i1[:nb] for a, nb in _CAND_ROWS] + pad_i, axis=0)
        best_s, experts = _top1_rounds(cand_s, cand_i, PEER_TOPK)
        ex = jnp.exp(best_s - jnp.max(best_s, axis=0, keepdims=True))
        e_ref[:, rows] = experts
        g_ref[:, rows] = ex / jnp.sum(ex, axis=0, keepdims=True)


def _peer_route(q, sub_keys, tt=256):
    n = q.shape[0]
    e, g = pl.pallas_call(
        functools.partial(_route_kernel, tt=tt),
        out_shape=[jax.ShapeDtypeStruct((PEER_HEADS, PEER_TOPK, n), jnp.int32),
                   jax.ShapeDtypeStruct((PEER_HEADS, PEER_TOPK, n), F32)],
        grid=(n // tt, PEER_HEADS),
        in_specs=[pl.BlockSpec((tt, 256), lambda i, h: (i, h)),
                  pl.BlockSpec((None, 2, PEER_KEYS, 128), lambda i, h: (h, 0, 0, 0))],
        out_specs=[pl.BlockSpec((None, PEER_TOPK, tt), lambda i, h: (h, 0, i)),
                   pl.BlockSpec((None, PEER_TOPK, tt), lambda i, h: (h, 0, i))],
        compiler_params=_cparams("parallel", "parallel"),
        name="peer_route",
    )(q, sub_keys)
    return e.reshape(PEER_SEL, n), g.reshape(PEER_SEL, n)


PEER_TB = 128
PEER_NBUF = 2
D_CHUNKS = D_MODEL // 128


def _peer_apply_kernel(ex_ref, gt_ref, h_ref, x_ref, mod_ref, uv_ref, o_ref, buf_ref, h32_ref, sem):
    h32_ref[...] = h_ref[...].astype(F32)

    def issue(t, slot):
        for k in range(PEER_SEL):
            row = pl.multiple_of(ex_ref[t, k] * D_CHUNKS, D_CHUNKS)
            pltpu.make_async_copy(uv_ref.at[pl.ds(row, D_CHUNKS)], buf_ref.at[slot, :, k],
                                  sem.at[slot]).start(priority=k % 2)

    def wait(slot):
        for r in range(D_CHUNKS):
            pltpu.make_async_copy(uv_ref.at[pl.ds(0, PEER_SEL)], buf_ref.at[slot, r], sem.at[slot]).wait()

    def chunk(slot, r):
        return buf_ref[slot, r]

    lane = lax.broadcasted_iota(jnp.int32, (1, PEER_TB), 1)

    def compute(t, slot):
        hrow = h32_ref[pl.ds(t, 1), :]
        acc = jnp.zeros((PEER_SEL, 128), F32)
        for r in range(D_CHUNKS):
            u = pltpu.bitcast(chunk(slot, r) & jnp.uint32(0xFFFF0000), F32)
            acc = acc + u * hrow[:, r * 128:(r + 1) * 128]
        s = jnp.sum(acc, axis=1, keepdims=True)
        gate = jnp.sum(jnp.where(lane == t, gt_ref[...], 0.0), axis=1, keepdims=True)
        act = _gelu(s) * gate
        outs = []
        for r in range(D_CHUNKS):
            v = pltpu.bitcast(chunk(slot, r) << 16, F32)
            outs.append(jnp.sum(v * act, axis=0, keepdims=True))
        out = jnp.concatenate(outs, axis=1)
        o_ref[pl.ds(t, 1), :] = x_ref[pl.ds(t, 1), :] + mod_ref[5:6, :] * out

    for slot in range(PEER_NBUF):
        issue(slot, slot)

    def body(i, carry):
        for slot in range(PEER_NBUF):
            t = i * PEER_NBUF + slot
            wait(slot)
            compute(t, slot)

            @pl.when(t + PEER_NBUF < PEER_TB)
            def _():
                issue(t + PEER_NBUF, slot)
        return carry

    lax.fori_loop(0, PEER_TB // PEER_NBUF, body, 0)


def _peer_apply(experts, gates_t, h2, x, mod, uv, *, seq):
    n, d = x.shape
    tb = PEER_TB
    return pl.pallas_call(
        _peer_apply_kernel,
        out_shape=jax.ShapeDtypeStruct((n, d), F32),
        grid=(n // tb,),
        in_specs=[pl.BlockSpec((tb, PEER_SEL), lambda i: (i, 0), memory_space=pltpu.SMEM),
                  pl.BlockSpec((PEER_SEL, tb), lambda i: (0, i)),
                  pl.BlockSpec((tb, d), lambda i: (i, 0)),
                  pl.BlockSpec((tb, d), lambda i: (i, 0)),
                  pl.BlockSpec((None, 6, d), lambda i: (i * tb // seq, 0, 0)),
                  pl.BlockSpec(memory_space=pl.ANY)],
        out_specs=pl.BlockSpec((tb, d), lambda i: (i, 0)),
        scratch_shapes=[pltpu.VMEM((PEER_NBUF, D_CHUNKS, PEER_SEL, 128), jnp.uint32),
                        pltpu.VMEM((tb, d), F32),
                        pltpu.SemaphoreType.DMA((PEER_NBUF,))],
        compiler_params=_cparams("arbitrary"),
        name="peer_apply",
    )(experts, gates_t, h2, x, mod, uv.reshape(-1, 128))


def _pack_uv(u_tab, v_tab):
    ub = lax.bitcast_convert_type(u_tab.astype(BF16), jnp.uint16).astype(jnp.uint32)
    vb = lax.bitcast_convert_type(v_tab.astype(BF16), jnp.uint16).astype(jnp.uint32)
    return (ub << 16) | vb


def _main_proj_weights(w_in):
    d = w_in.shape[0]
    main = jnp.concatenate([w_in[:, :4608], w_in[:, 4616:5448], w_in[:, 4608:4616],
                            jnp.zeros((d, P_WIDTH - 5448), w_in.dtype)], axis=1)
    return main.astype(BF16), w_in[:, 5448:].astype(BF16)


def _mla_weights(w_uq, w_ukv):
    qr = w_uq.shape[0]
    kvr = w_ukv.shape[0]
    wq = jnp.pad(w_uq.reshape(qr, N_HEADS, MLA_QK), ((0, 0), (0, 0), (0, MLA_PAD - MLA_QK))).reshape(qr, -1)
    kv = w_ukv.reshape(kvr, N_HEADS, MLA_NOPE + HEAD_W)
    k_nope = jnp.pad(kv[:, :, :MLA_NOPE], ((0, 0), (0, 0), (MLA_ROPE, MLA_PAD - MLA_QK))).reshape(kvr, -1)
    eye = jnp.pad(jnp.eye(MLA_ROPE, dtype=w_ukv.dtype), ((0, 128 - MLA_ROPE), (0, MLA_PAD - MLA_ROPE)))
    wk = jnp.concatenate([k_nope, jnp.tile(eye, (1, N_HEADS))], axis=0)
    wv = kv[:, :, MLA_NOPE:].reshape(kvr, -1)
    return wq.astype(BF16), wk.astype(BF16), wv.astype(BF16)


def kernel(x, c, positions, norm1_g, norm2_g, w_mod, b_mod, w_in, da_q_norm, da_k_norm, da_lambda, da_out_norm, sg_v_norm, sg_w, sg_b, ml_conv_w, ml_conv_b, ml_gate_b, ml_out_norm, mla_q_norm, mla_kv_norm, mla_w_uq, mla_w_ukv, mla_qk_norm, w_branch, w_out, peer_w_q, peer_sub_keys, peer_u, peer_v):
    batch, seq, d = x.shape
    depth = w_mod.shape[0]
    n = batch * seq
    xf = x.reshape(n, d)

    mod_all = _modulation(c, w_mod, b_mod)
    pos_col = positions.reshape(n, 1).astype(F32)
    cos_da, sin_da = _rope_tables(pos_col, _lane_freq_sign(DA_QK, DA_ROT))
    cos_ml, sin_ml = _rope_tables(pos_col, _lane_freq_sign(128, MLA_ROPE))

    for l in range(depth):
        mod = mod_all[l]
        lambda_init = 0.8 - 0.6 * math.exp(-0.3 * l)
        w_main, w_gates = _main_proj_weights(w_in[l])
        p = _norm_matmul(xf, norm1_g[l], mod, w_main, seq=seq, sc_row=1, sh_row=0, out_dtype=F32, tm=1024, tn=P_WIDTH // 4,
                         name="in_proj")
        gates = _norm_matmul(xf, norm1_g[l], mod, w_gates, seq=seq, sc_row=1, sh_row=0, out_dtype=BF16,
                             tm=1024, tn=1024, name="gate_proj")

        qn, kn = _da_prep(p, cos_da, sin_da, da_q_norm[l], da_k_norm[l])
        ya = _da_attn(qn, kn, p, da_lambda[l], da_out_norm[l], batch=batch, seq=seq, lambda_init=lambda_init)

        yb = _spatial_gating(p, sg_v_norm[l], sg_w[l], sg_b[l])

        qk = _ml_conv(p, ml_conv_w[l], ml_conv_b[l], seq=seq)
        yc = _mlstm(qk, p, ml_gate_b[l], ml_out_norm[l], batch=batch, seq=seq)

        wq, wk, wv = _mla_weights(mla_w_uq[l], mla_w_ukv[l])
        qk_g = jnp.pad(mla_qk_norm[l], ((0, 0), (0, MLA_PAD - MLA_QK)))
        qm, km, vm = _mla_prep(p, cos_ml, sin_ml, mla_q_norm[l], mla_kv_norm[l], wq, wk, wv, qk_g)
        yd = _mla_attn(qm, km, vm, batch=batch, seq=seq)

        mixed = _merge((ya, yb, yc, yd), gates, w_branch[l].astype(BF16))
        xf = _resid_matmul(mixed, w_out[l].astype(BF16), xf, mod, seq=seq, gate_row=2)

        q_peer, h2 = _norm_matmul(xf, norm2_g[l], mod, peer_w_q[l].astype(BF16), seq=seq, sc_row=4, sh_row=3,
                                  out_dtype=F32, emit_h=True, tm=1024, tn=1024, name="peer_query")
        experts_t, gates_t = _peer_route(q_peer, peer_sub_keys[l])
        uv = _pack_uv(peer_u[l], peer_v[l])
        xf = _peer_apply(experts_t.T, gates_t, h2, xf, mod, uv, seq=seq)

    return xf.reshape(batch, seq, d)
```

```python
import functools
import math

import numpy as np
import jax
import jax.numpy as jnp
from jax import lax
from jax.experimental import pallas as pl
from jax.experimental.pallas import tpu as pltpu

F32 = jnp.float32
BF16 = jnp.bfloat16

D_MODEL = 2048
CHUNK = 64
ROPE_THETA = 500000.0
RMS_EPS = 1e-6
N_HEADS = 4
HEAD_W = 128
BRANCH_W = N_HEADS * HEAD_W
DA_QK = 64
DA_ROT = 16
ML_CONV = 4
MLA_ROPE = 64
MLA_NOPE = 128
MLA_QK = MLA_ROPE + MLA_NOPE
MLA_PAD = 256
MLA_Q_RANK = 512
MLA_KV_RANK = 256
PEER_HEADS = 8
PEER_KEYS = 128
PEER_TOPK = 16
PEER_SEL = PEER_HEADS * PEER_TOPK

P_WIDTH = 11 * 512
COL_DAQ, COL_DAK, COL_DAV, COL_SGU, COL_SGV, COL_MLQK, COL_MLV, COL_MLO, COL_CQ, COL_MISC = 0, 1, 2, 3, 4, 5, 7, 8, 9, 10
MISC_KR = 256
MISC_GATE = 320

VMEM_LIMIT = 56 * 1024 * 1024
NEG_INF = float("-inf")
LOG2_E = math.log2(math.e)


def _cparams(*sem):
    return pltpu.CompilerParams(dimension_semantics=sem, vmem_limit_bytes=VMEM_LIMIT)


def _gelu(x):
    return 0.5 * x * (1.0 + lax.erf(x * (1.0 / math.sqrt(2.0))))


def _dot_nt(a, b):
    return lax.dot_general(a, b, (((1,), (1,)), ((), ())), preferred_element_type=F32)


def _mod_kernel(c_ref, w_ref, b_ref, o_ref):
    c = c_ref[...]
    cond = c * jax.nn.sigmoid(c)
    o_ref[0] = jnp.dot(cond, w_ref[0], preferred_element_type=F32,
                       precision=lax.Precision.HIGHEST) + b_ref[0]


def _modulation(c, w_mod, b_mod):
    depth, d, w6 = w_mod.shape
    b = c.shape[0]
    rows = 8
    cp = jnp.zeros((rows, d), F32).at[:b].set(c)
    tn = 1024
    out = pl.pallas_call(
        _mod_kernel,
        out_shape=jax.ShapeDtypeStruct((depth, rows, w6), F32),
        grid=(depth, w6 // tn),
        in_specs=[pl.BlockSpec((rows, d), lambda l, j: (0, 0)),
                  pl.BlockSpec((1, d, tn), lambda l, j: (l, 0, j)),
                  pl.BlockSpec((1, 1, tn), lambda l, j: (l, 0, j))],
        out_specs=pl.BlockSpec((1, rows, tn), lambda l, j: (l, 0, j)),
        compiler_params=_cparams("parallel", "parallel"),
        name="modulation",
    )(cp, w_mod, b_mod.reshape(depth, 1, w6))
    return out[:, :b].reshape(depth, b, 6, d)


def _nm_kernel(x_ref, g_ref, mod_ref, w_ref, *rest, sc_row, sh_row, emit_h):
    if emit_h:
        o_ref, h_out_ref, h_ref = rest
    else:
        o_ref, h_ref = rest

    @pl.when(pl.program_id(1) == 0)
    def _():
        x = x_ref[...]
        ms = jnp.mean(x * x, axis=-1, keepdims=True)
        y = x * lax.rsqrt(ms + RMS_EPS) * g_ref[...]
        h = y * (1.0 + mod_ref[sc_row:sc_row + 1, :]) + mod_ref[sh_row:sh_row + 1, :]
        h_ref[...] = h.astype(BF16)
        if emit_h:
            h_out_ref[...] = h.astype(BF16)

    o_ref[...] = jnp.dot(h_ref[...], w_ref[...], preferred_element_type=F32).astype(o_ref.dtype)


def _norm_matmul(x, g, mod, w, *, seq, sc_row, sh_row, out_dtype, emit_h=False, tm=512, tn=512, name):
    n, d = x.shape
    wn = w.shape[1]
    out_shape = [jax.ShapeDtypeStruct((n, wn), out_dtype)]
    out_specs = [pl.BlockSpec((tm, tn), lambda i, j: (i, j))]
    if emit_h:
        out_shape.append(jax.ShapeDtypeStruct((n, d), BF16))
        out_specs.append(pl.BlockSpec((tm, d), lambda i, j: (i, 0)))
    res = pl.pallas_call(
        functools.partial(_nm_kernel, sc_row=sc_row, sh_row=sh_row, emit_h=emit_h),
        out_shape=out_shape,
        grid=(n // tm, wn // tn),
        in_specs=[pl.BlockSpec((tm, d), lambda i, j: (i, 0)),
                  pl.BlockSpec((1, d), lambda i, j: (0, 0)),
                  pl.BlockSpec((None, 6, d), lambda i, j: (i * tm // seq, 0, 0)),
                  pl.BlockSpec((d, tn), lambda i, j: (0, j))],
        out_specs=out_specs,
        scratch_shapes=[pltpu.VMEM((tm, d), BF16)],
        compiler_params=_cparams("parallel", "arbitrary"),
        name=name,
    )(x, g.reshape(1, d), mod, w)
    return res if emit_h else res[0]


def _rope_table_kernel(pos_ref, f_ref, cos_ref, sin_ref):
    ang = pos_ref[...] * f_ref[0:1, :]
    cos_ref[...] = jnp.cos(ang)
    sin_ref[...] = jnp.sin(ang) * f_ref[1:2, :]


def _rope_tables(pos_col, freq_sign, tm=1024):
    n = pos_col.shape[0]
    return pl.pallas_call(
        _rope_table_kernel,
        out_shape=[jax.ShapeDtypeStruct((n, 128), F32)] * 2,
        grid=(n // tm,),
        in_specs=[pl.BlockSpec((tm, 1), lambda i: (i, 0)),
                  pl.BlockSpec((2, 128), lambda i: (0, 0))],
        out_specs=[pl.BlockSpec((tm, 128), lambda i: (i, 0))] * 2,
        compiler_params=_cparams("parallel"),
        name="rope_tables",
    )(pos_col, freq_sign)


def _lane_freq_sign(group, rot):
    half = rot // 2
    freq = ROPE_THETA ** (-jnp.arange(half, dtype=F32) / half)
    lane = np.arange(128) % group
    idx = np.where(lane < rot, lane % half, 0)
    f = jnp.where(jnp.asarray(lane < rot), freq[idx], 0.0)
    sign = np.where(lane < half, -1.0, np.where(lane < rot, 1.0, 0.0)).astype(np.float32)
    return jnp.stack([f, jnp.asarray(sign)], axis=0)


def _rope_apply(x, cos, sin, group, rot):
    half = rot // 2
    lane = lax.broadcasted_iota(jnp.int32, (1, 128), 1) % group
    partner = jnp.where(lane < half, pltpu.roll(x, 128 - half, 1), pltpu.roll(x, half, 1))
    return x * cos + partner * sin


def _da_prep_kernel(q_ref, k_ref, cos_ref, sin_ref, qg_ref, kg_ref, qo_ref, ko_ref):
    cos = cos_ref[...]
    sin = sin_ref[...]
    lo = lax.broadcasted_iota(jnp.int32, (1, 128), 1) < DA_QK

    def prep(ref, g_ref, o_ref, scale):
        for h in range(N_HEADS):
            x = ref[:, h * HEAD_W:(h + 1) * HEAD_W]
            sq = x * x
            s_lo = jnp.sum(jnp.where(lo, sq, 0.0), axis=-1, keepdims=True)
            s_hi = jnp.sum(jnp.where(lo, 0.0, sq), axis=-1, keepdims=True)
            ms = jnp.where(lo, s_lo, s_hi) * (1.0 / DA_QK)
            y = x * lax.rsqrt(ms + RMS_EPS) * g_ref[...]
            y = _rope_apply(y, cos, sin, DA_QK, DA_ROT)
            o_ref[:, h * HEAD_W:(h + 1) * HEAD_W] = (y * scale).astype(BF16)

    prep(q_ref, qg_ref, qo_ref, DA_QK ** -0.5 * LOG2_E)
    prep(k_ref, kg_ref, ko_ref, 1.0)


def _da_prep(p, cos, sin, q_g, k_g, tm=512):
    n = p.shape[0]
    return pl.pallas_call(
        _da_prep_kernel,
        out_shape=[jax.ShapeDtypeStruct((n, BRANCH_W), BF16)] * 2,
        grid=(n // tm,),
        in_specs=[pl.BlockSpec((tm, BRANCH_W), lambda i: (i, COL_DAQ)),
                  pl.BlockSpec((tm, BRANCH_W), lambda i: (i, COL_DAK)),
                  pl.BlockSpec((tm, 128), lambda i: (i, 0)),
                  pl.BlockSpec((tm, 128), lambda i: (i, 0)),
                  pl.BlockSpec((1, 128), lambda i: (0, 0)),
                  pl.BlockSpec((1, 128), lambda i: (0, 0))],
        out_specs=[pl.BlockSpec((tm, BRANCH_W), lambda i: (i, 0))] * 2,
        compiler_params=_cparams("parallel"),
        name="da_prep",
    )(p, p, cos, sin, q_g.reshape(1, 128), k_g.reshape(1, 128))


def _causal_pairs(nq, tq, tk):
    qi, ki, last = [], [], []
    for q in range(nq):
        nk = -(-((q + 1) * tq) // tk)
        for k in range(nk):
            qi.append(q)
            ki.append(k)
            last.append(int(k == nk - 1))
    return (jnp.asarray(qi, jnp.int32), jnp.asarray(ki, jnp.int32), jnp.asarray(last, jnp.int32))


def _attend(score_fns, values, mask, m_ref, l_ref, acc_ref):
    n = len(score_fns)
    scores, probs = {}, {}
    for i in range(n + 2):
        if i < n:
            s = score_fns[i]()
            scores[i] = s if mask is None else jnp.where(mask, s, NEG_INF)
        j = i - 1
        if 0 <= j < n:
            s = scores.pop(j)
            m_prev = m_ref[j]
            m_new = jnp.maximum(m_prev, jnp.max(s, axis=0, keepdims=True))
            alpha = jnp.exp2(m_prev - m_new)
            p = jnp.exp2(s - m_new)
            l_ref[j] = alpha * l_ref[j] + jnp.sum(p, axis=0, keepdims=True)
            m_ref[j] = m_new
            probs[j] = (p.astype(BF16), alpha)
        j = i - 2
        if 0 <= j < n:
            p, alpha = probs.pop(j)
            pv = lax.dot_general(values[j], p, (((0,), (0,)), ((), ())), preferred_element_type=F32)
            acc_ref[j] = alpha * acc_ref[j] + pv


def _chunk_mask_t(qi, ki, tq, tk):
    key = ki * tk + lax.broadcasted_iota(jnp.int32, (tk, 1), 0)
    qry = qi * tq + lax.broadcasted_iota(jnp.int32, (1, tq), 1)
    return (key // CHUNK) <= (qry // CHUNK)


def _init_softmax_state(m_ref, l_ref, acc_ref):
    m_ref[...] = jnp.full(m_ref.shape, NEG_INF, F32)
    l_ref[...] = jnp.zeros(l_ref.shape, F32)
    acc_ref[...] = jnp.zeros(acc_ref.shape, F32)


def _da_attn_kernel(qi_ref, ki_ref, last_ref, q_ref, k_ref, v_ref, lam_ref, og_ref, o_ref,
                    m_ref, l_ref, acc_ref, *, lambda_init, tq, tk):
    p_id = pl.program_id(1)
    qi = qi_ref[p_id]
    ki = ki_ref[p_id]

    @pl.when(ki == 0)
    def _():
        _init_softmax_state(m_ref, l_ref, acc_ref)

    lo = lax.broadcasted_iota(jnp.int32, (1, HEAD_W), 1) < DA_QK

    def attend(mask):
        score_fns, values = [], []
        for h in range(N_HEADS):
            sl = slice(h * HEAD_W, (h + 1) * HEAD_W)
            v = v_ref[:, sl].astype(BF16)
            for c in range(2):
                def score(sl=sl, c=c):
                    q = q_ref[:, sl]
                    zero = jnp.zeros_like(q)
                    return _dot_nt(k_ref[:, sl], jnp.where(lo, q, zero) if c == 0 else jnp.where(lo, zero, q))
                score_fns.append(score)
                values.append(v)
        _attend(score_fns, values, mask, m_ref, l_ref, acc_ref)

    @pl.when(ki == qi)
    def _():
        attend(_chunk_mask_t(qi, ki, tq, tk))

    @pl.when(ki != qi)
    def _():
        attend(None)

    @pl.when(last_ref[p_id] == 1)
    def _():
        lf = lam_ref[...]
        lam = (jnp.exp(jnp.sum(lf[0:1] * lf[1:2], axis=-1, keepdims=True))
               - jnp.exp(jnp.sum(lf[2:3] * lf[3:4], axis=-1, keepdims=True)) + lambda_init)
        for h in range(N_HEADS):
            o = acc_ref[2 * h] / l_ref[2 * h] - lam * (acc_ref[2 * h + 1] / l_ref[2 * h + 1])
            ms = jnp.mean(o * o, axis=0, keepdims=True)
            y = o * lax.rsqrt(ms + RMS_EPS) * og_ref[...] * (1.0 - lambda_init)
            o_ref[:, h * HEAD_W:(h + 1) * HEAD_W] = y.T.astype(o_ref.dtype)


def _da_attn(qn, kn, p, lam_vecs, out_g, *, batch, seq, lambda_init, tq=512, tk=512):
    assert tq == tk
    n = qn.shape[0]
    nq, nk = seq // tq, seq // tk
    qi, ki, last = _causal_pairs(nq, tq, tk)
    grid_spec = pltpu.PrefetchScalarGridSpec(
        num_scalar_prefetch=3,
        grid=(batch, int(qi.shape[0])),
        in_specs=[pl.BlockSpec((tq, BRANCH_W), lambda b, p_, qi_, ki_, la_: (b * nq + qi_[p_], 0)),
                  pl.BlockSpec((tk, BRANCH_W), lambda b, p_, qi_, ki_, la_: (b * nk + ki_[p_], 0)),
                  pl.BlockSpec((tk, BRANCH_W), lambda b, p_, qi_, ki_, la_: (b * nk + ki_[p_], COL_DAV)),
                  pl.BlockSpec((4, DA_QK), lambda b, p_, qi_, ki_, la_: (0, 0)),
                  pl.BlockSpec((HEAD_W, 1), lambda b, p_, qi_, ki_, la_: (0, 0))],
        out_specs=pl.BlockSpec((tq, BRANCH_W), lambda b, p_, qi_, ki_, la_: (b * nq + qi_[p_], 0)),
        scratch_shapes=[pltpu.VMEM((2 * N_HEADS, 1, tq), F32),
                        pltpu.VMEM((2 * N_HEADS, 1, tq), F32),
                        pltpu.VMEM((2 * N_HEADS, HEAD_W, tq), F32)],
    )
    return pl.pallas_call(
        functools.partial(_da_attn_kernel, lambda_init=lambda_init, tq=tq, tk=tk),
        out_shape=jax.ShapeDtypeStruct((n, BRANCH_W), BF16),
        grid_spec=grid_spec,
        compiler_params=_cparams("parallel", "arbitrary"),
        name="da_attn",
    )(qi, ki, last, qn, kn, p, lam_vecs, out_g.reshape(HEAD_W, 1))


def _sg_kernel(u_ref, v_ref, vg_ref, w_ref, b_ref, o_ref, *, tm):
    idx_i = lax.broadcasted_iota(jnp.int32, (128, 128), 0)
    idx_j = lax.broadcasted_iota(jnp.int32, (128, 128), 1)
    wmask = (idx_j // CHUNK) <= (idx_i // CHUNK)
    for g in range(N_HEADS):
        sl = slice(g * HEAD_W, (g + 1) * HEAD_W)
        u = _gelu(u_ref[:, sl])
        v = _gelu(v_ref[:, sl])
        ms = jnp.mean(v * v, axis=-1, keepdims=True)
        vn = (v * lax.rsqrt(ms + RMS_EPS) * vg_ref[g:g + 1, :]).astype(BF16)
        w = jnp.where(wmask, w_ref[g], 0.0).astype(BF16)
        bias = b_ref[:, g:g + 1]
        for nb in range(tm // 128):
            rows = slice(nb * 128, (nb + 1) * 128)
            sv = jnp.dot(w, vn[rows], preferred_element_type=F32) + bias
            o_ref[rows, sl] = (u[rows] * sv).astype(o_ref.dtype)


def _spatial_gating(p, v_g, w_s, b_s, tm=512):
    n = p.shape[0]
    return pl.pallas_call(
        functools.partial(_sg_kernel, tm=tm),
        out_shape=jax.ShapeDtypeStruct((n, BRANCH_W), BF16),
        grid=(n // tm,),
        in_specs=[pl.BlockSpec((tm, BRANCH_W), lambda i: (i, COL_SGU)),
                  pl.BlockSpec((tm, BRANCH_W), lambda i: (i, COL_SGV)),
                  pl.BlockSpec((N_HEADS, 128), lambda i: (0, 0)),
                  pl.BlockSpec((N_HEADS, 128, 128), lambda i: (0, 0, 0)),
                  pl.BlockSpec((128, N_HEADS), lambda i: (0, 0))],
        out_specs=pl.BlockSpec((tm, BRANCH_W), lambda i: (i, 0)),
        compiler_params=_cparams("parallel"),
        name="spatial_gating",
    )(p, p, v_g, w_s, b_s.T)


def _ml_conv_kernel(x_ref, halo_ref, w_ref, b_ref, o_ref, *, tm, seq):
    i = pl.program_id(0)
    first = (i * tm) % seq == 0
    halo = jnp.where(first, 0.0, halo_ref[...])
    xc = jnp.concatenate([halo, x_ref[...]], axis=0)
    acc = jnp.zeros((tm, x_ref.shape[1]), F32) + b_ref[...]
    for j in range(ML_CONV):
        shift = ML_CONV - 1 - j
        xs = xc if shift == 0 else pltpu.roll(xc, shift, 0)
        acc = acc + xs[8:8 + tm] * w_ref[j:j + 1, :]
    o_ref[...] = acc * jax.nn.sigmoid(acc)


def _ml_conv(p, conv_w, conv_b, *, seq, tm=512):
    n = p.shape[0]
    width = 2 * BRANCH_W
    cb = width // BRANCH_W
    return pl.pallas_call(
        functools.partial(_ml_conv_kernel, tm=tm, seq=seq),
        out_shape=jax.ShapeDtypeStruct((n, width), F32),
        grid=(n // tm, cb),
        in_specs=[pl.BlockSpec((tm, BRANCH_W), lambda i, j: (i, COL_MLQK + j)),
                  pl.BlockSpec((8, BRANCH_W), lambda i, j: (jnp.maximum(i * (tm // 8) - 1, 0), COL_MLQK + j)),
                  pl.BlockSpec((ML_CONV, BRANCH_W), lambda i, j: (0, j)),
                  pl.BlockSpec((1, BRANCH_W), lambda i, j: (0, j))],
        out_specs=pl.BlockSpec((tm, BRANCH_W), lambda i, j: (i, j)),
        compiler_params=_cparams("parallel", "parallel"),
        name="ml_conv",
    )(p, p, conv_w, conv_b.reshape(1, width))


def _log_sigmoid(x):
    return jnp.minimum(x, 0.0) - jnp.log1p(jnp.exp(-jnp.abs(x)))


def _mlstm_kernel(q_ref, k_ref, v_ref, o_ref, gc_ref, gr_ref, bc_ref, br_ref, og_ref, y_ref,
                  c_ref, n_ref, m_ref, *, batch):
    @pl.when(pl.program_id(0) == 0)
    def _():
        c_ref[...] = jnp.zeros(c_ref.shape, F32)
        n_ref[...] = jnp.zeros(n_ref.shape, F32)
        m_ref[...] = jnp.zeros(m_ref.shape, F32)

    L = CHUNK
    it = lax.broadcasted_iota(jnp.int32, (L, L), 0)
    is_ = lax.broadcasted_iota(jnp.int32, (L, L), 1)
    tril = is_ <= it
    triu = it <= is_
    chains = [(b, h) for b in range(batch) for h in range(N_HEADS)]
    ins = []
    for b, h in chains:
        st = b * N_HEADS + h
        sl = slice(h * HEAD_W, (h + 1) * HEAD_W)
        ins.append((q_ref[b, :, sl], k_ref[b, :, sl], v_ref[b, :, sl], o_ref[b, :, sl],
                    gc_ref[b, :, MISC_GATE:MISC_GATE + 2 * N_HEADS] + bc_ref[...], gr_ref[b] + br_ref[...],
                    c_ref[st], n_ref[st], m_ref[st][:, 0:1]))
    each = lambda f, *cols: [f(*xs) for xs in zip(*cols)]
    hs = [h for _, h in chains]
    q32, k, v, o, gcol, grow, c_prev, n_prev, m_prev = (list(col) for col in zip(*ins))
    q = each(lambda x: x.astype(BF16), q32)
    k = each(lambda x: x * (HEAD_W ** -0.5), k)
    kb = each(lambda x: x.astype(BF16), k)
    vb = each(lambda x: x.astype(BF16), v)
    li_row = each(lambda gr, h: gr[h:h + 1, :], grow, hs)
    lf_row = each(lambda gr, h: _log_sigmoid(gr[N_HEADS + h:N_HEADS + h + 1, :]), grow, hs)
    li_col = each(lambda gc, h: gc[:, h:h + 1], gcol, hs)
    lf_col = each(lambda gc, h: _log_sigmoid(gc[:, N_HEADS + h:N_HEADS + h + 1]), gcol, hs)
    bcum_col = each(lambda x: jnp.sum(jnp.where(tril, x, 0.0), axis=1, keepdims=True), lf_row)
    bcum_row = each(lambda x: jnp.sum(jnp.where(triu, x, 0.0), axis=0, keepdims=True), lf_col)
    g = each(lambda x: jnp.sum(x, axis=1, keepdims=True), lf_row)
    dmat = each(lambda bc, br, li: jnp.where(tril, bc - br + li, NEG_INF), bcum_col, bcum_row, li_row)
    m_inter = each(lambda bc, m: bc + m, bcum_col, m_prev)
    m_t = each(lambda mi, d: jnp.maximum(mi, jnp.max(d, axis=1, keepdims=True)), m_inter, dmat)
    a_inter = each(lambda mi, mt: jnp.exp(mi - mt), m_inter, m_t)
    qk = each(_dot_nt, q, kb)
    sm = each(lambda d, mt, s_: jnp.exp(d - mt) * s_, dmat, m_t, qk)
    qc = each(lambda q_, c: jnp.dot(q_, c.astype(BF16), preferred_element_type=F32), q, c_prev)
    sv = each(lambda s_, v_: jnp.dot(s_.astype(BF16), v_, preferred_element_type=F32), sm, vb)
    num = each(lambda ai, x, y_: ai * x + y_, a_inter, qc, sv)
    qn = each(lambda q_, n_: jnp.sum(q_ * n_, axis=1, keepdims=True), q32, n_prev)
    den = each(lambda ai, x, s_: ai * x + jnp.sum(s_, axis=1, keepdims=True), a_inter, qn, sm)
    h_t = each(lambda nu, de, mt: nu / jnp.maximum(jnp.abs(de), jnp.exp(-mt)), num, den, m_t)
    dec_row = each(lambda g_, br, li: g_ - br + li, g, bcum_row, li_row)
    dec_col = each(lambda g_, bc, li: g_ - bc + li, g, bcum_col, li_col)
    m_new = each(lambda g_, m, d: jnp.maximum(g_ + m, jnp.max(d, axis=1, keepdims=True)), g, m_prev, dec_row)
    w_col = each(lambda d, m: jnp.exp(d - m), dec_col, m_new)
    a = each(lambda g_, m, mn: jnp.exp(g_ + m - mn), g, m_prev, m_new)
    kw = each(lambda w, k_: w * k_, w_col, k)
    kv = each(lambda kw_, v_: lax.dot_general(kw_.astype(BF16), v_, (((0,), (0,)), ((), ())),
                                              preferred_element_type=F32), kw, vb)
    c_new = each(lambda a_, c, x: a_ * c + x, a, c_prev, kv)
    n_new = each(lambda a_, n_, kw_: a_ * n_ + jnp.sum(kw_, axis=0, keepdims=True), a, n_prev, kw)
    y = each(lambda o_, h_: jax.nn.sigmoid(o_) * h_, o, h_t)
    ms = each(lambda y_: jnp.mean(y_ * y_, axis=-1, keepdims=True), y)
    y = each(lambda y_, ms_, h: (y_ * lax.rsqrt(ms_ + RMS_EPS) * og_ref[h:h + 1, :]).astype(y_ref.dtype), y, ms, hs)
    for (b, h), c_, n_, m_, y_ in zip(chains, c_new, n_new, m_new, y):
        st = b * N_HEADS + h
        c_ref[st] = c_
        n_ref[st] = n_
        m_ref[st] = jnp.broadcast_to(m_, (1, HEAD_W))
        y_ref[b, :, h * HEAD_W:(h + 1) * HEAD_W] = y_


def _mlstm(qk, p, gate_b, out_g, *, batch, seq):
    n = p.shape[0]
    nc = seq // CHUNK
    g8 = p[:, COL_MISC * 512 + MISC_GATE:COL_MISC * 512 + MISC_GATE + 2 * N_HEADS]
    grow = jnp.swapaxes(g8.reshape(batch, nc, CHUNK, 2 * N_HEADS), 2, 3)
    bias = gate_b.reshape(1, 2 * N_HEADS)
    qk3 = qk.reshape(batch, seq, qk.shape[1])
    p3 = p.reshape(batch, seq, p.shape[1])
    blk = lambda col: pl.BlockSpec((batch, CHUNK, BRANCH_W), functools.partial(lambda c, col_: (0, c, col_), col_=col))
    y = pl.pallas_call(
        functools.partial(_mlstm_kernel, batch=batch),
        out_shape=jax.ShapeDtypeStruct((batch, seq, BRANCH_W), BF16),
        grid=(nc,),
        in_specs=[blk(0), blk(1), blk(COL_MLV), blk(COL_MLO), blk(COL_MISC),
                  pl.BlockSpec((batch, None, 2 * N_HEADS, CHUNK), lambda c: (0, c, 0, 0)),
                  pl.BlockSpec((1, 2 * N_HEADS), lambda c: (0, 0)),
                  pl.BlockSpec((2 * N_HEADS, 1), lambda c: (0, 0)),
                  pl.BlockSpec((N_HEADS, HEAD_W), lambda c: (0, 0))],
        out_specs=pl.BlockSpec((batch, CHUNK, BRANCH_W), lambda c: (0, c, 0)),
        scratch_shapes=[pltpu.VMEM((batch * N_HEADS, HEAD_W, HEAD_W), F32),
                        pltpu.VMEM((batch * N_HEADS, 1, HEAD_W), F32),
                        pltpu.VMEM((batch * N_HEADS, 1, HEAD_W), F32)],
        compiler_params=_cparams("arbitrary"),
        name="mlstm",
    )(qk3, qk3, p3, p3, p3, grow, bias, bias.reshape(2 * N_HEADS, 1), out_g)
    return y.reshape(n, BRANCH_W)


def _mla_prep_kernel(cq_ref, misc_ref, cos_ref, sin_ref, qa_ref, kva_ref, wq_ref, wk_ref, wv_ref, qkg_ref,
                     qo_ref, ko_ref, vo_ref):
    def rms(x, g):
        ms = jnp.mean(x * x, axis=-1, keepdims=True)
        return x * lax.rsqrt(ms + RMS_EPS) * g

    cqn = rms(cq_ref[...], qa_ref[...]).astype(BF16)
    ckvn = rms(misc_ref[:, :MLA_KV_RANK], kva_ref[...]).astype(BF16)
    lane = lax.broadcasted_iota(jnp.int32, (1, 128), 1)
    kr = jnp.where(lane < MLA_ROPE, misc_ref[:, MISC_KR:MISC_KR + 128], 0.0).astype(BF16)
    q_lin = jnp.dot(cqn, wq_ref[...], preferred_element_type=F32)
    k_lin = jnp.dot(jnp.concatenate([ckvn, kr], axis=1), wk_ref[...], preferred_element_type=F32)
    vo_ref[...] = jnp.dot(ckvn, wv_ref[...], preferred_element_type=F32).astype(vo_ref.dtype)
    cos = cos_ref[...]
    sin = sin_ref[...]

    def finish(x_lin, g_row, o_ref, scale):
        for h in range(N_HEADS):
            x = x_lin[:, h * MLA_PAD:(h + 1) * MLA_PAD]
            ms = jnp.sum(x * x, axis=-1, keepdims=True) * (1.0 / MLA_QK)
            y = x * lax.rsqrt(ms + RMS_EPS) * qkg_ref[g_row:g_row + 1, :]
            y0 = _rope_apply(y[:, :128], cos, sin, 128, MLA_ROPE)
            o_ref[:, h * MLA_PAD:h * MLA_PAD + 128] = (y0 * scale).astype(o_ref.dtype)
            o_ref[:, h * MLA_PAD + 128:(h + 1) * MLA_PAD] = (y[:, 128:] * scale).astype(o_ref.dtype)

    finish(q_lin, 0, qo_ref, MLA_QK ** -0.5 * LOG2_E)
    finish(k_lin, 1, ko_ref, 1.0)


def _mla_prep(p, cos, sin, q_a_g, kv_a_g, wq, wk, wv, qk_g, tm=512):
    n = p.shape[0]
    hw = N_HEADS * MLA_PAD
    full = lambda shape: pl.BlockSpec(shape, lambda i: (0,) * len(shape))
    return pl.pallas_call(
        _mla_prep_kernel,
        out_shape=[jax.ShapeDtypeStruct((n, hw), BF16), jax.ShapeDtypeStruct((n, hw), BF16),
                   jax.ShapeDtypeStruct((n, BRANCH_W), BF16)],
        grid=(n // tm,),
        in_specs=[pl.BlockSpec((tm, 512), lambda i: (i, COL_CQ)),
                  pl.BlockSpec((tm, 512), lambda i: (i, COL_MISC)),
                  pl.BlockSpec((tm, 128), lambda i: (i, 0)),
                  pl.BlockSpec((tm, 128), lambda i: (i, 0)),
                  full((1, MLA_Q_RANK)), full((1, MLA_KV_RANK)),
                  full(wq.shape), full(wk.shape), full(wv.shape), full((2, MLA_PAD))],
        out_specs=[pl.BlockSpec((tm, hw), lambda i: (i, 0)), pl.BlockSpec((tm, hw), lambda i: (i, 0)),
                   pl.BlockSpec((tm, BRANCH_W), lambda i: (i, 0))],
        compiler_params=_cparams("parallel"),
        name="mla_prep",
    )(p, p, cos, sin, q_a_g.reshape(1, -1), kv_a_g.reshape(1, -1), wq, wk, wv, qk_g)


def _mla_attn_kernel(qi_ref, ki_ref, last_ref, q_ref, k_ref, v_ref, o_ref, m_ref, l_ref, acc_ref, *, tq, tk):
    p_id = pl.program_id(1)
    qi = qi_ref[p_id]
    ki = ki_ref[p_id]

    @pl.when(ki == 0)
    def _():
        _init_softmax_state(m_ref, l_ref, acc_ref)

    def attend(mask):
        score_fns = [functools.partial(lambda sl: _dot_nt(k_ref[:, sl], q_ref[:, sl]),
                                       slice(h * MLA_PAD, (h + 1) * MLA_PAD)) for h in range(N_HEADS)]
        values = [v_ref[:, h * HEAD_W:(h + 1) * HEAD_W] for h in range(N_HEADS)]
        _attend(score_fns, values, mask, m_ref, l_ref, acc_ref)

    @pl.when(ki == qi)
    def _():
        attend(_chunk_mask_t(qi, ki, tq, tk))

    @pl.when(ki != qi)
    def _():
        attend(None)

    @pl.when(last_ref[p_id] == 1)
    def _():
        for h in range(N_HEADS):
            o_ref[:, h * HEAD_W:(h + 1) * HEAD_W] = (acc_ref[h] / l_ref[h]).T.astype(o_ref.dtype)


def _mla_attn(qm, km, vm, *, batch, seq, tq=512, tk=512):
    assert tq == tk
    n = qm.shape[0]
    hw = N_HEADS * MLA_PAD
    nq, nk = seq // tq, seq // tk
    qi, ki, last = _causal_pairs(nq, tq, tk)
    grid_spec = pltpu.PrefetchScalarGridSpec(
        num_scalar_prefetch=3,
        grid=(batch, int(qi.shape[0])),
        in_specs=[pl.BlockSpec((tq, hw), lambda b, p_, qi_, ki_, la_: (b * nq + qi_[p_], 0)),
                  pl.BlockSpec((tk, hw), lambda b, p_, qi_, ki_, la_: (b * nk + ki_[p_], 0)),
                  pl.BlockSpec((tk, BRANCH_W), lambda b, p_, qi_, ki_, la_: (b * nk + ki_[p_], 0))],
        out_specs=pl.BlockSpec((tq, BRANCH_W), lambda b, p_, qi_, ki_, la_: (b * nq + qi_[p_], 0)),
        scratch_shapes=[pltpu.VMEM((N_HEADS, 1, tq), F32),
                        pltpu.VMEM((N_HEADS, 1, tq), F32),
                        pltpu.VMEM((N_HEADS, HEAD_W, tq), F32)],
    )
    return pl.pallas_call(
        functools.partial(_mla_attn_kernel, tq=tq, tk=tk),
        out_shape=jax.ShapeDtypeStruct((n, BRANCH_W), BF16),
        grid_spec=grid_spec,
        compiler_params=_cparams("parallel", "arbitrary"),
        name="mla_attn",
    )(qi, ki, last, qm, km, vm)


def _merge_kernel(ya_ref, yb_ref, yc_ref, yd_ref, g0_ref, g1_ref, g2_ref, g3_ref, wb_ref, o_ref):
    acc = None
    for n, (y_ref, g_ref) in enumerate(((ya_ref, g0_ref), (yb_ref, g1_ref), (yc_ref, g2_ref), (yd_ref, g3_ref))):
        up = jnp.dot(y_ref[...], wb_ref[n], preferred_element_type=F32)
        term = jax.nn.sigmoid(g_ref[...].astype(F32)) * up
        acc = term if acc is None else acc + term
    o_ref[...] = acc.astype(o_ref.dtype)


def _merge(ys, gates, wb, tm=1024, tn=512):
    n = gates.shape[0]
    nj = D_MODEL // tn
    y_spec = pl.BlockSpec((tm, BRANCH_W), lambda i, j: (i, 0))
    g_specs = [pl.BlockSpec((tm, tn), functools.partial(lambda i, j, nn: (i, nn * nj + j), nn=nn)) for nn in range(4)]
    return pl.pallas_call(
        _merge_kernel,
        out_shape=jax.ShapeDtypeStruct((n, D_MODEL), BF16),
        grid=(n // tm, nj),
        in_specs=[y_spec] * 4 + g_specs + [pl.BlockSpec((4, BRANCH_W, tn), lambda i, j: (0, 0, j))],
        out_specs=pl.BlockSpec((tm, tn), lambda i, j: (i, j)),
        compiler_params=_cparams("parallel", "arbitrary"),
        name="merge",
    )(*ys, gates, gates, gates, gates, wb)


def _resid_matmul_kernel(m_ref, w_ref, x_ref, mod_ref, o_ref, *, gate_row):
    y = jnp.dot(m_ref[...], w_ref[...], preferred_element_type=F32)
    o_ref[...] = x_ref[...] + mod_ref[gate_row:gate_row + 1, :] * y


def _resid_matmul(m, w, x, mod, *, seq, gate_row, tm=1024, tn=512):
    n, d = x.shape
    kdim = m.shape[1]
    return pl.pallas_call(
        functools.partial(_resid_matmul_kernel, gate_row=gate_row),
        out_shape=jax.ShapeDtypeStruct((n, d), F32),
        grid=(n // tm, d // tn),
        in_specs=[pl.BlockSpec((tm, kdim), lambda i, j: (i, 0)),
                  pl.BlockSpec((kdim, tn), lambda i, j: (0, j)),
                  pl.BlockSpec((tm, tn), lambda i, j: (i, j)),
                  pl.BlockSpec((None, 6, tn), lambda i, j: (i * tm // seq, 0, j))],
        out_specs=pl.BlockSpec((tm, tn), lambda i, j: (i, j)),
        compiler_params=_cparams("parallel", "arbitrary"),
        name="out_proj",
    )(m, w, x, mod)


def _top1_rounds(problems, rounds):
    state = []
    for s, ids in problems:
        row = lax.broadcasted_iota(jnp.int32, s.shape, 0).astype(F32)
        state.append([s, ids, row, float(s.shape[0]), [], []])
    for _ in range(rounds):
        ms = [jnp.max(st[0], axis=0, keepdims=True) for st in state]
        poss = [jnp.min(jnp.where(st[0] == m, st[2], st[3]), axis=0, keepdims=True) for st, m in zip(state, ms)]
        for st, m, pos in zip(state, ms, poss):
            sel = st[2] == pos
            st[4].append(m)
            st[5].append(pos if st[1] is None else jnp.sum(jnp.where(sel, st[1], 0), axis=0, keepdims=True))
            st[0] = jnp.where(sel, NEG_INF, st[0])
    res = []
    for s, ids, row, nrow, vals, outs in state:
        out = jnp.concatenate(outs, axis=0)
        res.append((jnp.concatenate(vals, axis=0), out.astype(jnp.int32) if ids is None else out))
    return res


_CAND_ROWS = [(a, PEER_TOPK // (a + 1)) for a in range(PEER_TOPK)]
_CAND_PAD = -sum(nb for _, nb in _CAND_ROWS) % 8


def _route_kernel(q_ref, keys_ref, e_ref, g_ref, *, tt):
    subs = [slice(sub * 128, (sub + 1) * 128) for sub in range(tt // 128)]
    scores = []
    for rows in subs:
        for c in range(2):
            qc = q_ref[rows, c * 128:(c + 1) * 128]
            sc = lax.dot_general(keys_ref[c], qc, (((1,), (1,)), ((), ())), preferred_element_type=F32,
                                 precision=lax.Precision.HIGHEST)
            scores.append((sc, None))
    tops = _top1_rounds(scores, PEER_TOPK)
    cands = []
    for j in range(len(subs)):
        (s0, i0), (s1, i1) = tops[2 * j], tops[2 * j + 1]
        pad_s = [jnp.full((_CAND_PAD, 128), NEG_INF, F32)] if _CAND_PAD else []
        pad_i = [jnp.zeros((_CAND_PAD, 128), jnp.int32)] if _CAND_PAD else []
        cand_s = jnp.concatenate([s0[a:a + 1] + s1[:nb] for a, nb in _CAND_ROWS] + pad_s, axis=0)
        cand_i = jnp.concatenate([i0[a:a + 1] * PEER_KEYS + i1[:nb] for a, nb in _CAND_ROWS] + pad_i, axis=0)
        cands.append((cand_s, cand_i))
    for rows, (best_s, experts) in zip(subs, _top1_rounds(cands, PEER_TOPK)):
        ex = jnp.exp(best_s - jnp.max(best_s, axis=0, keepdims=True))
        e_ref[:, rows] = experts
        g_ref[:, rows] = ex / jnp.sum(ex, axis=0, keepdims=True)


def _peer_route(q, sub_keys, tt=256):
    n = q.shape[0]
    e, g = pl.pallas_call(
        functools.partial(_route_kernel, tt=tt),
        out_shape=[jax.ShapeDtypeStruct((PEER_HEADS, PEER_TOPK, n), jnp.int32),
                   jax.ShapeDtypeStruct((PEER_HEADS, PEER_TOPK, n), F32)],
        grid=(n // tt, PEER_HEADS),
        in_specs=[pl.BlockSpec((tt, 256), lambda i, h: (i, h)),
                  pl.BlockSpec((None, 2, PEER_KEYS, 128), lambda i, h: (h, 0, 0, 0))],
        out_specs=[pl.BlockSpec((None, PEER_TOPK, tt), lambda i, h: (h, 0, i)),
                   pl.BlockSpec((None, PEER_TOPK, tt), lambda i, h: (h, 0, i))],
        compiler_params=_cparams("parallel", "parallel"),
        name="peer_route",
    )(q, sub_keys)
    return e.reshape(PEER_SEL, n), g.reshape(PEER_SEL, n)


PEER_TB = 128
PEER_NBUF = 4
D_CHUNKS = D_MODEL // 128


def _peer_apply_kernel(ex_ref, gt_ref, h_ref, x_ref, mod_ref, uv_ref, o_ref, buf_ref, h32_ref, sem):
    h32_ref[...] = h_ref[...].astype(F32)

    def issue(t, slot):
        for k in range(PEER_SEL):
            row = pl.multiple_of(ex_ref[t, k] * D_CHUNKS, D_CHUNKS)
            pltpu.make_async_copy(uv_ref.at[pl.ds(row, D_CHUNKS)], buf_ref.at[slot, :, k],
                                  sem.at[slot]).start(priority=k % 2)

    def wait(slot):
        for r in range(D_CHUNKS):
            pltpu.make_async_copy(uv_ref.at[pl.ds(0, PEER_SEL)], buf_ref.at[slot, r], sem.at[slot]).wait()

    def chunk(slot, r):
        return buf_ref[slot, r]

    lane = lax.broadcasted_iota(jnp.int32, (1, PEER_TB), 1)

    def compute(t, slot):
        acc = jnp.zeros((PEER_SEL, 128), F32)
        for r in range(D_CHUNKS):
            u = pltpu.bitcast(chunk(slot, r) & jnp.uint32(0xFFFF0000), F32)
            acc = acc + u * h32_ref[t, r:r + 1, :]
        s = jnp.sum(acc, axis=1, keepdims=True)
        gate = jnp.sum(jnp.where(lane == t, gt_ref[...], 0.0), axis=1, keepdims=True)
        act = _gelu(s) * gate
        outs = []
        for r in range(D_CHUNKS):
            v = pltpu.bitcast(chunk(slot, r) << 16, F32)
            outs.append(jnp.sum(v * act, axis=0, keepdims=True))
        out = jnp.concatenate(outs, axis=1)
        o_ref[pl.ds(t, 1), :] = x_ref[pl.ds(t, 1), :] + mod_ref[5:6, :] * out

    for slot in range(PEER_NBUF):
        issue(slot, slot)

    def body(i, carry):
        for slot in range(PEER_NBUF):
            t = i * PEER_NBUF + slot
            wait(slot)
            compute(t, slot)

            @pl.when(t + PEER_NBUF < PEER_TB)
            def _():
                issue(t + PEER_NBUF, slot)
        return carry

    lax.fori_loop(0, PEER_TB // PEER_NBUF, body, 0)


def _peer_apply(experts, gates_t, h2, x, mod, uv, *, seq):
    n, d = x.shape
    tb = PEER_TB
    return pl.pallas_call(
        _peer_apply_kernel,
        out_shape=jax.ShapeDtypeStruct((n, d), F32),
        grid=(n // tb,),
        in_specs=[pl.BlockSpec((tb, PEER_SEL), lambda i: (i, 0), memory_space=pltpu.SMEM),
                  pl.BlockSpec((PEER_SEL, tb), lambda i: (0, i)),
                  pl.BlockSpec((tb, D_CHUNKS, 128), lambda i: (i, 0, 0)),
                  pl.BlockSpec((tb, d), lambda i: (i, 0)),
                  pl.BlockSpec((None, 6, d), lambda i: (i * tb // seq, 0, 0)),
                  pl.BlockSpec(memory_space=pl.ANY)],
        out_specs=pl.BlockSpec((tb, d), lambda i: (i, 0)),
        scratch_shapes=[pltpu.VMEM((PEER_NBUF, D_CHUNKS, PEER_SEL, 128), jnp.uint32),
                        pltpu.VMEM((tb, D_CHUNKS, 128), F32),
                        pltpu.SemaphoreType.DMA((PEER_NBUF,))],
        compiler_params=_cparams("arbitrary"),
        name="peer_apply",
    )(experts, gates_t, h2.reshape(n, D_CHUNKS, 128), x, mod, uv.reshape(-1, 128))


def _pack_uv(u_tab, v_tab):
    ub = lax.bitcast_convert_type(u_tab.astype(BF16), jnp.uint16).astype(jnp.uint32)
    vb = lax.bitcast_convert_type(v_tab.astype(BF16), jnp.uint16).astype(jnp.uint32)
    return (ub << 16) | vb


def _main_proj_weights(w_in):
    d = w_in.shape[0]
    main = jnp.concatenate([w_in[:, :4608], w_in[:, 4616:5448], w_in[:, 4608:4616],
                            jnp.zeros((d, P_WIDTH - 5448), w_in.dtype)], axis=1)
    return main.astype(BF16), w_in[:, 5448:].astype(BF16)


def _mla_weights(w_uq, w_ukv):
    qr = w_uq.shape[0]
    kvr = w_ukv.shape[0]
    wq = jnp.pad(w_uq.reshape(qr, N_HEADS, MLA_QK), ((0, 0), (0, 0), (0, MLA_PAD - MLA_QK))).reshape(qr, -1)
    kv = w_ukv.reshape(kvr, N_HEADS, MLA_NOPE + HEAD_W)
    k_nope = jnp.pad(kv[:, :, :MLA_NOPE], ((0, 0), (0, 0), (MLA_ROPE, MLA_PAD - MLA_QK))).reshape(kvr, -1)
    eye = jnp.pad(jnp.eye(MLA_ROPE, dtype=w_ukv.dtype), ((0, 128 - MLA_ROPE), (0, MLA_PAD - MLA_ROPE)))
    wk = jnp.concatenate([k_nope, jnp.tile(eye, (1, N_HEADS))], axis=0)
    wv = kv[:, :, MLA_NOPE:].reshape(kvr, -1)
    return wq.astype(BF16), wk.astype(BF16), wv.astype(BF16)


def kernel(x, c, positions, norm1_g, norm2_g, w_mod, b_mod, w_in, da_q_norm, da_k_norm, da_lambda, da_out_norm, sg_v_norm, sg_w, sg_b, ml_conv_w, ml_conv_b, ml_gate_b, ml_out_norm, mla_q_norm, mla_kv_norm, mla_w_uq, mla_w_ukv, mla_qk_norm, w_branch, w_out, peer_w_q, peer_sub_keys, peer_u, peer_v):
    batch, seq, d = x.shape
    depth = w_mod.shape[0]
    n = batch * seq
    xf = x.reshape(n, d)

    mod_all = _modulation(c, w_mod, b_mod)
    pos_col = positions.reshape(n, 1).astype(F32)
    cos_da, sin_da = _rope_tables(pos_col, _lane_freq_sign(DA_QK, DA_ROT))
    cos_ml, sin_ml = _rope_tables(pos_col, _lane_freq_sign(128, MLA_ROPE))

    for l in range(depth):
        mod = mod_all[l]
        lambda_init = 0.8 - 0.6 * math.exp(-0.3 * l)
        w_main, w_gates = _main_proj_weights(w_in[l])
        p = _norm_matmul(xf, norm1_g[l], mod, w_main, seq=seq, sc_row=1, sh_row=0, out_dtype=F32, tm=1024, tn=P_WIDTH // 4,
                         name="in_proj")
        gates = _norm_matmul(xf, norm1_g[l], mod, w_gates, seq=seq, sc_row=1, sh_row=0, out_dtype=BF16,
                             tm=1024, tn=1024, name="gate_proj")

        qn, kn = _da_prep(p, cos_da, sin_da, da_q_norm[l], da_k_norm[l])
        ya = _da_attn(qn, kn, p, da_lambda[l], da_out_norm[l], batch=batch, seq=seq, lambda_init=lambda_init)

        yb = _spatial_gating(p, sg_v_norm[l], sg_w[l], sg_b[l])

        qk = _ml_conv(p, ml_conv_w[l], ml_conv_b[l], seq=seq)
        yc = _mlstm(qk, p, ml_gate_b[l], ml_out_norm[l], batch=batch, seq=seq)

        wq, wk, wv = _mla_weights(mla_w_uq[l], mla_w_ukv[l])
        qk_g = jnp.pad(mla_qk_norm[l], ((0, 0), (0, MLA_PAD - MLA_QK)))
        qm, km, vm = _mla_prep(p, cos_ml, sin_ml, mla_q_norm[l], mla_kv_norm[l], wq, wk, wv, qk_g)
        yd = _mla_attn(qm, km, vm, batch=batch, seq=seq)

        mixed = _merge((ya, yb, yc, yd), gates, w_branch[l].astype(BF16))
        xf = _resid_matmul(mixed, w_out[l].astype(BF16), xf, mod, seq=seq, gate_row=2)

        q_peer, h2 = _norm_matmul(xf, norm2_g[l], mod, peer_w_q[l].astype(BF16), seq=seq, sc_row=4, sh_row=3,
                                  out_dtype=F32, emit_h=True, tm=1024, tn=1024, name="peer_query")
        experts_t, gates_t = _peer_route(q_peer, peer_sub_keys[l])
        uv = _pack_uv(peer_u[l], peer_v[l])
        xf = _peer_apply(experts_t.T, gates_t, h2, xf, mod, uv, seq=seq)

    return xf.reshape(batch, seq, d)
```

```python
import functools
import math

import numpy as np
import jax
import jax.numpy as jnp
from jax import lax
from jax.experimental import pallas as pl
from jax.experimental.pallas import tpu as pltpu

F32 = jnp.float32
BF16 = jnp.bfloat16

D_MODEL = 2048
CHUNK = 64
ROPE_THETA = 500000.0
RMS_EPS = 1e-6
N_HEADS = 4
HEAD_W = 128
BRANCH_W = N_HEADS * HEAD_W
DA_QK = 64
DA_ROT = 16
ML_CONV = 4
MLA_ROPE = 64
MLA_NOPE = 128
MLA_QK = MLA_ROPE + MLA_NOPE
MLA_PAD = 256
MLA_Q_RANK = 512
MLA_KV_RANK = 256
PEER_HEADS = 8
PEER_KEYS = 128
PEER_TOPK = 16
PEER_SEL = PEER_HEADS * PEER_TOPK

P_WIDTH = 11 * 512
COL_DAQ, COL_DAK, COL_DAV, COL_SGU, COL_SGV, COL_MLQK, COL_MLV, COL_MLO, COL_CQ, COL_MISC = 0, 1, 2, 3, 4, 5, 7, 8, 9, 10
MISC_KR = 256
MISC_GATE = 320

VMEM_LIMIT = 56 * 1024 * 1024
NEG_INF = float("-inf")
LOG2_E = math.log2(math.e)


def _cparams(*sem):
    return pltpu.CompilerParams(dimension_semantics=sem, vmem_limit_bytes=VMEM_LIMIT)


def _gelu(x):
    return 0.5 * x * (1.0 + lax.erf(x * (1.0 / math.sqrt(2.0))))


def _dot_nt(a, b):
    return lax.dot_general(a, b, (((1,), (1,)), ((), ())), preferred_element_type=F32)


def _mod_kernel(c_ref, w_ref, b_ref, o_ref):
    c = c_ref[...]
    cond = c * jax.nn.sigmoid(c)
    o_ref[0] = jnp.dot(cond, w_ref[0], preferred_element_type=F32,
                       precision=lax.Precision.HIGHEST) + b_ref[0]


def _modulation(c, w_mod, b_mod):
    depth, d, w6 = w_mod.shape
    b = c.shape[0]
    rows = 8
    cp = jnp.zeros((rows, d), F32).at[:b].set(c)
    tn = 1024
    out = pl.pallas_call(
        _mod_kernel,
        out_shape=jax.ShapeDtypeStruct((depth, rows, w6), F32),
        grid=(depth, w6 // tn),
        in_specs=[pl.BlockSpec((rows, d), lambda l, j: (0, 0)),
                  pl.BlockSpec((1, d, tn), lambda l, j: (l, 0, j)),
                  pl.BlockSpec((1, 1, tn), lambda l, j: (l, 0, j))],
        out_specs=pl.BlockSpec((1, rows, tn), lambda l, j: (l, 0, j)),
        compiler_params=_cparams("parallel", "parallel"),
        name="modulation",
    )(cp, w_mod, b_mod.reshape(depth, 1, w6))
    return out[:, :b].reshape(depth, b, 6, d)


def _nm_kernel(x_ref, g_ref, mod_ref, w_ref, *rest, sc_row, sh_row, emit_h):
    if emit_h:
        o_ref, h_out_ref, h_ref = rest
    else:
        o_ref, h_ref = rest

    @pl.when(pl.program_id(1) == 0)
    def _():
        x = x_ref[...]
        ms = jnp.mean(x * x, axis=-1, keepdims=True)
        y = x * lax.rsqrt(ms + RMS_EPS) * g_ref[...]
        h = y * (1.0 + mod_ref[sc_row:sc_row + 1, :]) + mod_ref[sh_row:sh_row + 1, :]
        h_ref[...] = h.astype(BF16)
        if emit_h:
            h_out_ref[...] = h.astype(BF16)

    o_ref[...] = jnp.dot(h_ref[...], w_ref[...], preferred_element_type=F32).astype(o_ref.dtype)


def _norm_matmul(x, g, mod, w, *, seq, sc_row, sh_row, out_dtype, emit_h=False, tm=512, tn=512, name):
    n, d = x.shape
    wn = w.shape[1]
    out_shape = [jax.ShapeDtypeStruct((n, wn), out_dtype)]
    out_specs = [pl.BlockSpec((tm, tn), lambda i, j: (i, j))]
    if emit_h:
        out_shape.append(jax.ShapeDtypeStruct((n, d), BF16))
        out_specs.append(pl.BlockSpec((tm, d), lambda i, j: (i, 0)))
    res = pl.pallas_call(
        functools.partial(_nm_kernel, sc_row=sc_row, sh_row=sh_row, emit_h=emit_h),
        out_shape=out_shape,
        grid=(n // tm, wn // tn),
        in_specs=[pl.BlockSpec((tm, d), lambda i, j: (i, 0)),
                  pl.BlockSpec((1, d), lambda i, j: (0, 0)),
                  pl.BlockSpec((None, 6, d), lambda i, j: (i * tm // seq, 0, 0)),
                  pl.BlockSpec((d, tn), lambda i, j: (0, j))],
        out_specs=out_specs,
        scratch_shapes=[pltpu.VMEM((tm, d), BF16)],
        compiler_params=_cparams("parallel", "arbitrary"),
        name=name,
    )(x, g.reshape(1, d), mod, w)
    return res if emit_h else res[0]


def _rope_table_kernel(pos_ref, f_ref, cos_ref, sin_ref):
    ang = pos_ref[...] * f_ref[0:1, :]
    cos_ref[...] = jnp.cos(ang)
    sin_ref[...] = jnp.sin(ang) * f_ref[1:2, :]


def _rope_tables(pos_col, freq_sign, tm=1024):
    n = pos_col.shape[0]
    return pl.pallas_call(
        _rope_table_kernel,
        out_shape=[jax.ShapeDtypeStruct((n, 128), F32)] * 2,
        grid=(n // tm,),
        in_specs=[pl.BlockSpec((tm, 1), lambda i: (i, 0)),
                  pl.BlockSpec((2, 128), lambda i: (0, 0))],
        out_specs=[pl.BlockSpec((tm, 128), lambda i: (i, 0))] * 2,
        compiler_params=_cparams("parallel"),
        name="rope_tables",
    )(pos_col, freq_sign)


def _lane_freq_sign(group, rot):
    half = rot // 2
    freq = ROPE_THETA ** (-jnp.arange(half, dtype=F32) / half)
    lane = np.arange(128) % group
    idx = np.where(lane < rot, lane % half, 0)
    f = jnp.where(jnp.asarray(lane < rot), freq[idx], 0.0)
    sign = np.where(lane < half, -1.0, np.where(lane < rot, 1.0, 0.0)).astype(np.float32)
    return jnp.stack([f, jnp.asarray(sign)], axis=0)


def _rope_apply(x, cos, sin, group, rot):
    half = rot // 2
    lane = lax.broadcasted_iota(jnp.int32, (1, 128), 1) % group
    partner = jnp.where(lane < half, pltpu.roll(x, 128 - half, 1), pltpu.roll(x, half, 1))
    return x * cos + partner * sin


def _da_prep_kernel(q_ref, k_ref, cos_ref, sin_ref, qg_ref, kg_ref, qo_ref, ko_ref):
    cos = cos_ref[...]
    sin = sin_ref[...]
    lo = lax.broadcasted_iota(jnp.int32, (1, 128), 1) < DA_QK

    def prep(ref, g_ref, o_ref, scale):
        for h in range(N_HEADS):
            x = ref[:, h * HEAD_W:(h + 1) * HEAD_W]
            sq = x * x
            s_lo = jnp.sum(jnp.where(lo, sq, 0.0), axis=-1, keepdims=True)
            s_hi = jnp.sum(jnp.where(lo, 0.0, sq), axis=-1, keepdims=True)
            ms = jnp.where(lo, s_lo, s_hi) * (1.0 / DA_QK)
            y = x * lax.rsqrt(ms + RMS_EPS) * g_ref[...]
            y = _rope_apply(y, cos, sin, DA_QK, DA_ROT)
            o_ref[:, h * HEAD_W:(h + 1) * HEAD_W] = (y * scale).astype(BF16)

    prep(q_ref, qg_ref, qo_ref, DA_QK ** -0.5 * LOG2_E)
    prep(k_ref, kg_ref, ko_ref, 1.0)


def _da_prep(p, cos, sin, q_g, k_g, tm=512):
    n = p.shape[0]
    return pl.pallas_call(
        _da_prep_kernel,
        out_shape=[jax.ShapeDtypeStruct((n, BRANCH_W), BF16)] * 2,
        grid=(n // tm,),
        in_specs=[pl.BlockSpec((tm, BRANCH_W), lambda i: (i, COL_DAQ)),
                  pl.BlockSpec((tm, BRANCH_W), lambda i: (i, COL_DAK)),
                  pl.BlockSpec((tm, 128), lambda i: (i, 0)),
                  pl.BlockSpec((tm, 128), lambda i: (i, 0)),
                  pl.BlockSpec((1, 128), lambda i: (0, 0)),
                  pl.BlockSpec((1, 128), lambda i: (0, 0))],
        out_specs=[pl.BlockSpec((tm, BRANCH_W), lambda i: (i, 0))] * 2,
        compiler_params=_cparams("parallel"),
        name="da_prep",
    )(p, p, cos, sin, q_g.reshape(1, 128), k_g.reshape(1, 128))


def _causal_pairs(nq, tq, tk):
    qi, ki, last = [], [], []
    for q in range(nq):
        nk = -(-((q + 1) * tq) // tk)
        for k in range(nk):
            qi.append(q)
            ki.append(k)
            last.append(int(k == nk - 1))
    return (jnp.asarray(qi, jnp.int32), jnp.asarray(ki, jnp.int32), jnp.asarray(last, jnp.int32))


def _attend(score_fns, values, mask, m_ref, l_ref, acc_ref):
    n = len(score_fns)
    scores, probs = {}, {}
    for i in range(n + 2):
        if i < n:
            s = score_fns[i]()
            scores[i] = s if mask is None else jnp.where(mask, s, NEG_INF)
        j = i - 1
        if 0 <= j < n:
            s = scores.pop(j)
            m_prev = m_ref[j]
            m_new = jnp.maximum(m_prev, jnp.max(s, axis=0, keepdims=True))
            alpha = jnp.exp2(m_prev - m_new)
            p = jnp.exp2(s - m_new)
            l_ref[j] = alpha * l_ref[j] + jnp.sum(p, axis=0, keepdims=True)
            m_ref[j] = m_new
            probs[j] = (p.astype(BF16), alpha)
        j = i - 2
        if 0 <= j < n:
            p, alpha = probs.pop(j)
            pv = lax.dot_general(values[j], p, (((0,), (0,)), ((), ())), preferred_element_type=F32)
            acc_ref[j] = alpha * acc_ref[j] + pv


def _chunk_mask_t(qi, ki, tq, tk):
    key = ki * tk + lax.broadcasted_iota(jnp.int32, (tk, 1), 0)
    qry = qi * tq + lax.broadcasted_iota(jnp.int32, (1, tq), 1)
    return (key // CHUNK) <= (qry // CHUNK)


def _init_softmax_state(m_ref, l_ref, acc_ref):
    m_ref[...] = jnp.full(m_ref.shape, NEG_INF, F32)
    l_ref[...] = jnp.zeros(l_ref.shape, F32)
    acc_ref[...] = jnp.zeros(acc_ref.shape, F32)


def _da_attn_kernel(qi_ref, ki_ref, last_ref, q_ref, k_ref, v_ref, lam_ref, og_ref, o_ref,
                    m_ref, l_ref, acc_ref, *, lambda_init, tq, tk):
    p_id = pl.program_id(1)
    qi = qi_ref[p_id]
    ki = ki_ref[p_id]

    @pl.when(ki == 0)
    def _():
        _init_softmax_state(m_ref, l_ref, acc_ref)

    lo = lax.broadcasted_iota(jnp.int32, (1, HEAD_W), 1) < DA_QK

    def attend(mask):
        score_fns, values = [], []
        for h in range(N_HEADS):
            sl = slice(h * HEAD_W, (h + 1) * HEAD_W)
            v = v_ref[:, sl].astype(BF16)
            for c in range(2):
                def score(sl=sl, c=c):
                    q = q_ref[:, sl]
                    zero = jnp.zeros_like(q)
                    return _dot_nt(k_ref[:, sl], jnp.where(lo, q, zero) if c == 0 else jnp.where(lo, zero, q))
                score_fns.append(score)
                values.append(v)
        _attend(score_fns, values, mask, m_ref, l_ref, acc_ref)

    @pl.when(ki == qi)
    def _():
        attend(_chunk_mask_t(qi, ki, tq, tk))

    @pl.when(ki != qi)
    def _():
        attend(None)

    @pl.when(last_ref[p_id] == 1)
    def _():
        lf = lam_ref[...]
        lam = (jnp.exp(jnp.sum(lf[0:1] * lf[1:2], axis=-1, keepdims=True))
               - jnp.exp(jnp.sum(lf[2:3] * lf[3:4], axis=-1, keepdims=True)) + lambda_init)
        for h in range(N_HEADS):
            o = acc_ref[2 * h] / l_ref[2 * h] - lam * (acc_ref[2 * h + 1] / l_ref[2 * h + 1])
            ms = jnp.mean(o * o, axis=0, keepdims=True)
            y = o * lax.rsqrt(ms + RMS_EPS) * og_ref[...] * (1.0 - lambda_init)
            o_ref[:, h * HEAD_W:(h + 1) * HEAD_W] = y.T.astype(o_ref.dtype)


def _da_attn(qn, kn, p, lam_vecs, out_g, *, batch, seq, lambda_init, tq=1024, tk=1024):
    assert tq == tk
    n = qn.shape[0]
    nq, nk = seq // tq, seq // tk
    qi, ki, last = _causal_pairs(nq, tq, tk)
    grid_spec = pltpu.PrefetchScalarGridSpec(
        num_scalar_prefetch=3,
        grid=(batch, int(qi.shape[0])),
        in_specs=[pl.BlockSpec((tq, BRANCH_W), lambda b, p_, qi_, ki_, la_: (b * nq + qi_[p_], 0)),
                  pl.BlockSpec((tk, BRANCH_W), lambda b, p_, qi_, ki_, la_: (b * nk + ki_[p_], 0)),
                  pl.BlockSpec((tk, BRANCH_W), lambda b, p_, qi_, ki_, la_: (b * nk + ki_[p_], COL_DAV)),
                  pl.BlockSpec((4, DA_QK), lambda b, p_, qi_, ki_, la_: (0, 0)),
                  pl.BlockSpec((HEAD_W, 1), lambda b, p_, qi_, ki_, la_: (0, 0))],
        out_specs=pl.BlockSpec((tq, BRANCH_W), lambda b, p_, qi_, ki_, la_: (b * nq + qi_[p_], 0)),
        scratch_shapes=[pltpu.VMEM((2 * N_HEADS, 1, tq), F32),
                        pltpu.VMEM((2 * N_HEADS, 1, tq), F32),
                        pltpu.VMEM((2 * N_HEADS, HEAD_W, tq), F32)],
    )
    return pl.pallas_call(
        functools.partial(_da_attn_kernel, lambda_init=lambda_init, tq=tq, tk=tk),
        out_shape=jax.ShapeDtypeStruct((n, BRANCH_W), BF16),
        grid_spec=grid_spec,
        compiler_params=_cparams("parallel", "arbitrary"),
        name="da_attn",
    )(qi, ki, last, qn, kn, p, lam_vecs, out_g.reshape(HEAD_W, 1))


def _sg_kernel(u_ref, v_ref, vg_ref, w_ref, b_ref, o_ref, *, tm):
    idx_i = lax.broadcasted_iota(jnp.int32, (128, 128), 0)
    idx_j = lax.broadcasted_iota(jnp.int32, (128, 128), 1)
    wmask = (idx_j // CHUNK) <= (idx_i // CHUNK)
    for g in range(N_HEADS):
        sl = slice(g * HEAD_W, (g + 1) * HEAD_W)
        u = _gelu(u_ref[:, sl])
        v = _gelu(v_ref[:, sl])
        ms = jnp.mean(v * v, axis=-1, keepdims=True)
        vn = (v * lax.rsqrt(ms + RMS_EPS) * vg_ref[g:g + 1, :]).astype(BF16)
        w = jnp.where(wmask, w_ref[g], 0.0).astype(BF16)
        bias = b_ref[:, g:g + 1]
        for nb in range(tm // 128):
            rows = slice(nb * 128, (nb + 1) * 128)
            sv = jnp.dot(w, vn[rows], preferred_element_type=F32) + bias
            o_ref[rows, sl] = (u[rows] * sv).astype(o_ref.dtype)


def _spatial_gating(p, v_g, w_s, b_s, tm=512):
    n = p.shape[0]
    return pl.pallas_call(
        functools.partial(_sg_kernel, tm=tm),
        out_shape=jax.ShapeDtypeStruct((n, BRANCH_W), BF16),
        grid=(n // tm,),
        in_specs=[pl.BlockSpec((tm, BRANCH_W), lambda i: (i, COL_SGU)),
                  pl.BlockSpec((tm, BRANCH_W), lambda i: (i, COL_SGV)),
                  pl.BlockSpec((N_HEADS, 128), lambda i: (0, 0)),
                  pl.BlockSpec((N_HEADS, 128, 128), lambda i: (0, 0, 0)),
                  pl.BlockSpec((128, N_HEADS), lambda i: (0, 0))],
        out_specs=pl.BlockSpec((tm, BRANCH_W), lambda i: (i, 0)),
        compiler_params=_cparams("parallel"),
        name="spatial_gating",
    )(p, p, v_g, w_s, b_s.T)


def _ml_conv_kernel(x_ref, halo_ref, w_ref, b_ref, o_ref, *, tm, seq):
    i = pl.program_id(0)
    first = (i * tm) % seq == 0
    halo = jnp.where(first, 0.0, halo_ref[...])
    xc = jnp.concatenate([halo, x_ref[...]], axis=0)
    acc = jnp.zeros((tm, x_ref.shape[1]), F32) + b_ref[...]
    for j in range(ML_CONV):
        shift = ML_CONV - 1 - j
        xs = xc if shift == 0 else pltpu.roll(xc, shift, 0)
        acc = acc + xs[8:8 + tm] * w_ref[j:j + 1, :]
    o_ref[...] = acc * jax.nn.sigmoid(acc)


def _ml_conv(p, conv_w, conv_b, *, seq, tm=512):
    n = p.shape[0]
    width = 2 * BRANCH_W
    cb = width // BRANCH_W
    return pl.pallas_call(
        functools.partial(_ml_conv_kernel, tm=tm, seq=seq),
        out_shape=jax.ShapeDtypeStruct((n, width), F32),
        grid=(n // tm, cb),
        in_specs=[pl.BlockSpec((tm, BRANCH_W), lambda i, j: (i, COL_MLQK + j)),
                  pl.BlockSpec((8, BRANCH_W), lambda i, j: (jnp.maximum(i * (tm // 8) - 1, 0), COL_MLQK + j)),
                  pl.BlockSpec((ML_CONV, BRANCH_W), lambda i, j: (0, j)),
                  pl.BlockSpec((1, BRANCH_W), lambda i, j: (0, j))],
        out_specs=pl.BlockSpec((tm, BRANCH_W), lambda i, j: (i, j)),
        compiler_params=_cparams("parallel", "parallel"),
        name="ml_conv",
    )(p, p, conv_w, conv_b.reshape(1, width))


def _log_sigmoid(x):
    return jnp.minimum(x, 0.0) - jnp.log1p(jnp.exp(-jnp.abs(x)))


def _mlstm_kernel(q_ref, k_ref, v_ref, o_ref, gc_ref, gr_ref, bc_ref, br_ref, og_ref, y_ref,
                  c_ref, n_ref, m_ref, *, batch):
    @pl.when(pl.program_id(0) == 0)
    def _():
        c_ref[...] = jnp.zeros(c_ref.shape, F32)
        n_ref[...] = jnp.zeros(n_ref.shape, F32)
        m_ref[...] = jnp.zeros(m_ref.shape, F32)

    L = CHUNK
    it = lax.broadcasted_iota(jnp.int32, (L, L), 0)
    is_ = lax.broadcasted_iota(jnp.int32, (L, L), 1)
    tril = is_ <= it
    triu = it <= is_
    chains = [(b, h) for b in range(batch) for h in range(N_HEADS)]
    ins = []
    for b, h in chains:
        st = b * N_HEADS + h
        sl = slice(h * HEAD_W, (h + 1) * HEAD_W)
        ins.append((q_ref[b, :, sl], k_ref[b, :, sl], v_ref[b, :, sl], o_ref[b, :, sl],
                    gc_ref[b, :, MISC_GATE:MISC_GATE + 2 * N_HEADS] + bc_ref[...], gr_ref[b] + br_ref[...],
                    c_ref[st], n_ref[st], m_ref[st][:, 0:1]))
    each = lambda f, *cols: [f(*xs) for xs in zip(*cols)]
    hs = [h for _, h in chains]
    q32, k, v, o, gcol, grow, c_prev, n_prev, m_prev = (list(col) for col in zip(*ins))
    q = each(lambda x: x.astype(BF16), q32)
    k = each(lambda x: x * (HEAD_W ** -0.5), k)
    kb = each(lambda x: x.astype(BF16), k)
    vb = each(lambda x: x.astype(BF16), v)
    li_row = each(lambda gr, h: gr[h:h + 1, :], grow, hs)
    lf_row = each(lambda gr, h: _log_sigmoid(gr[N_HEADS + h:N_HEADS + h + 1, :]), grow, hs)
    li_col = each(lambda gc, h: gc[:, h:h + 1], gcol, hs)
    lf_col = each(lambda gc, h: _log_sigmoid(gc[:, N_HEADS + h:N_HEADS + h + 1]), gcol, hs)
    bcum_col = each(lambda x: jnp.sum(jnp.where(tril, x, 0.0), axis=1, keepdims=True), lf_row)
    bcum_row = each(lambda x: jnp.sum(jnp.where(triu, x, 0.0), axis=0, keepdims=True), lf_col)
    g = each(lambda x: jnp.sum(x, axis=1, keepdims=True), lf_row)
    dmat = each(lambda bc, br, li: jnp.where(tril, bc - br + li, NEG_INF), bcum_col, bcum_row, li_row)
    m_inter = each(lambda bc, m: bc + m, bcum_col, m_prev)
    m_t = each(lambda mi, d: jnp.maximum(mi, jnp.max(d, axis=1, keepdims=True)), m_inter, dmat)
    a_inter = each(lambda mi, mt: jnp.exp(mi - mt), m_inter, m_t)
    qk = each(_dot_nt, q, kb)
    sm = each(lambda d, mt, s_: jnp.exp(d - mt) * s_, dmat, m_t, qk)
    qc = each(lambda q_, c: jnp.dot(q_, c.astype(BF16), preferred_element_type=F32), q, c_prev)
    sv = each(lambda s_, v_: jnp.dot(s_.astype(BF16), v_, preferred_element_type=F32), sm, vb)
    num = each(lambda ai, x, y_: ai * x + y_, a_inter, qc, sv)
    qn = each(lambda q_, n_: jnp.sum(q_ * n_, axis=1, keepdims=True), q32, n_prev)
    den = each(lambda ai, x, s_: ai * x + jnp.sum(s_, axis=1, keepdims=True), a_inter, qn, sm)
    h_t = each(lambda nu, de, mt: nu / jnp.maximum(jnp.abs(de), jnp.exp(-mt)), num, den, m_t)
    dec_row = each(lambda g_, br, li: g_ - br + li, g, bcum_row, li_row)
    dec_col = each(lambda g_, bc, li: g_ - bc + li, g, bcum_col, li_col)
    m_new = each(lambda g_, m, d: jnp.maximum(g_ + m, jnp.max(d, axis=1, keepdims=True)), g, m_prev, dec_row)
    w_col = each(lambda d, m: jnp.exp(d - m), dec_col, m_new)
    a = each(lambda g_, m, mn: jnp.exp(g_ + m - mn), g, m_prev, m_new)
    kw = each(lambda w, k_: w * k_, w_col, k)
    kv = each(lambda kw_, v_: lax.dot_general(kw_.astype(BF16), v_, (((0,), (0,)), ((), ())),
                                              preferred_element_type=F32), kw, vb)
    c_new = each(lambda a_, c, x: a_ * c + x, a, c_prev, kv)
    n_new = each(lambda a_, n_, kw_: a_ * n_ + jnp.sum(kw_, axis=0, keepdims=True), a, n_prev, kw)
    y = each(lambda o_, h_: jax.nn.sigmoid(o_) * h_, o, h_t)
    ms = each(lambda y_: jnp.mean(y_ * y_, axis=-1, keepdims=True), y)
    y = each(lambda y_, ms_, h: (y_ * lax.rsqrt(ms_ + RMS_EPS) * og_ref[h:h + 1, :]).astype(y_ref.dtype), y, ms, hs)
    for (b, h), c_, n_, m_, y_ in zip(chains, c_new, n_new, m_new, y):
        st = b * N_HEADS + h
        c_ref[st] = c_
        n_ref[st] = n_
        m_ref[st] = jnp.broadcast_to(m_, (1, HEAD_W))
        y_ref[b, :, h * HEAD_W:(h + 1) * HEAD_W] = y_


def _mlstm(qk, p, gate_b, out_g, *, batch, seq):
    n = p.shape[0]
    nc = seq // CHUNK
    g8 = p[:, COL_MISC * 512 + MISC_GATE:COL_MISC * 512 + MISC_GATE + 2 * N_HEADS]
    grow = jnp.swapaxes(g8.reshape(batch, nc, CHUNK, 2 * N_HEADS), 2, 3)
    bias = gate_b.reshape(1, 2 * N_HEADS)
    qk3 = qk.reshape(batch, seq, qk.shape[1])
    p3 = p.reshape(batch, seq, p.shape[1])
    blk = lambda col: pl.BlockSpec((batch, CHUNK, BRANCH_W), functools.partial(lambda c, col_: (0, c, col_), col_=col))
    y = pl.pallas_call(
        functools.partial(_mlstm_kernel, batch=batch),
        out_shape=jax.ShapeDtypeStruct((batch, seq, BRANCH_W), BF16),
        grid=(nc,),
        in_specs=[blk(0), blk(1), blk(COL_MLV), blk(COL_MLO), blk(COL_MISC),
                  pl.BlockSpec((batch, None, 2 * N_HEADS, CHUNK), lambda c: (0, c, 0, 0)),
                  pl.BlockSpec((1, 2 * N_HEADS), lambda c: (0, 0)),
                  pl.BlockSpec((2 * N_HEADS, 1), lambda c: (0, 0)),
                  pl.BlockSpec((N_HEADS, HEAD_W), lambda c: (0, 0))],
        out_specs=pl.BlockSpec((batch, CHUNK, BRANCH_W), lambda c: (0, c, 0)),
        scratch_shapes=[pltpu.VMEM((batch * N_HEADS, HEAD_W, HEAD_W), F32),
                        pltpu.VMEM((batch * N_HEADS, 1, HEAD_W), F32),
                        pltpu.VMEM((batch * N_HEADS, 1, HEAD_W), F32)],
        compiler_params=_cparams("arbitrary"),
        name="mlstm",
    )(qk3, qk3, p3, p3, p3, grow, bias, bias.reshape(2 * N_HEADS, 1), out_g)
    return y.reshape(n, BRANCH_W)


def _mla_prep_kernel(cq_ref, misc_ref, cos_ref, sin_ref, qa_ref, kva_ref, wq_ref, wk_ref, wv_ref, qkg_ref,
                     qo_ref, ko_ref, vo_ref):
    def rms(x, g):
        ms = jnp.mean(x * x, axis=-1, keepdims=True)
        return x * lax.rsqrt(ms + RMS_EPS) * g

    cqn = rms(cq_ref[...], qa_ref[...]).astype(BF16)
    ckvn = rms(misc_ref[:, :MLA_KV_RANK], kva_ref[...]).astype(BF16)
    lane = lax.broadcasted_iota(jnp.int32, (1, 128), 1)
    kr = jnp.where(lane < MLA_ROPE, misc_ref[:, MISC_KR:MISC_KR + 128], 0.0).astype(BF16)
    q_lin = jnp.dot(cqn, wq_ref[...], preferred_element_type=F32)
    k_lin = jnp.dot(jnp.concatenate([ckvn, kr], axis=1), wk_ref[...], preferred_element_type=F32)
    vo_ref[...] = jnp.dot(ckvn, wv_ref[...], preferred_element_type=F32).astype(vo_ref.dtype)
    cos = cos_ref[...]
    sin = sin_ref[...]

    def finish(x_lin, g_row, o_ref, scale):
        for h in range(N_HEADS):
            x = x_lin[:, h * MLA_PAD:(h + 1) * MLA_PAD]
            ms = jnp.sum(x * x, axis=-1, keepdims=True) * (1.0 / MLA_QK)
            y = x * lax.rsqrt(ms + RMS_EPS) * qkg_ref[g_row:g_row + 1, :]
            y0 = _rope_apply(y[:, :128], cos, sin, 128, MLA_ROPE)
            o_ref[:, h * MLA_PAD:h * MLA_PAD + 128] = (y0 * scale).astype(o_ref.dtype)
            o_ref[:, h * MLA_PAD + 128:(h + 1) * MLA_PAD] = (y[:, 128:] * scale).astype(o_ref.dtype)

    finish(q_lin, 0, qo_ref, MLA_QK ** -0.5 * LOG2_E)
    finish(k_lin, 1, ko_ref, 1.0)


def _mla_prep(p, cos, sin, q_a_g, kv_a_g, wq, wk, wv, qk_g, tm=512):
    n = p.shape[0]
    hw = N_HEADS * MLA_PAD
    full = lambda shape: pl.BlockSpec(shape, lambda i: (0,) * len(shape))
    return pl.pallas_call(
        _mla_prep_kernel,
        out_shape=[jax.ShapeDtypeStruct((n, hw), BF16), jax.ShapeDtypeStruct((n, hw), BF16),
                   jax.ShapeDtypeStruct((n, BRANCH_W), BF16)],
        grid=(n // tm,),
        in_specs=[pl.BlockSpec((tm, 512), lambda i: (i, COL_CQ)),
                  pl.BlockSpec((tm, 512), lambda i: (i, COL_MISC)),
                  pl.BlockSpec((tm, 128), lambda i: (i, 0)),
                  pl.BlockSpec((tm, 128), lambda i: (i, 0)),
                  full((1, MLA_Q_RANK)), full((1, MLA_KV_RANK)),
                  full(wq.shape), full(wk.shape), full(wv.shape), full((2, MLA_PAD))],
        out_specs=[pl.BlockSpec((tm, hw), lambda i: (i, 0)), pl.BlockSpec((tm, hw), lambda i: (i, 0)),
                   pl.BlockSpec((tm, BRANCH_W), lambda i: (i, 0))],
        compiler_params=_cparams("parallel"),
        name="mla_prep",
    )(p, p, cos, sin, q_a_g.reshape(1, -1), kv_a_g.reshape(1, -1), wq, wk, wv, qk_g)


def _mla_attn_kernel(qi_ref, ki_ref, last_ref, q_ref, k_ref, v_ref, o_ref, m_ref, l_ref, acc_ref, *, tq, tk):
    p_id = pl.program_id(1)
    qi = qi_ref[p_id]
    ki = ki_ref[p_id]

    @pl.when(ki == 0)
    def _():
        _init_softmax_state(m_ref, l_ref, acc_ref)

    def attend(mask):
        score_fns = [functools.partial(lambda sl: _dot_nt(k_ref[:, sl], q_ref[:, sl]),
                                       slice(h * MLA_PAD, (h + 1) * MLA_PAD)) for h in range(N_HEADS)]
        values = [v_ref[:, h * HEAD_W:(h + 1) * HEAD_W] for h in range(N_HEADS)]
        _attend(score_fns, values, mask, m_ref, l_ref, acc_ref)

    @pl.when(ki == qi)
    def _():
        attend(_chunk_mask_t(qi, ki, tq, tk))

    @pl.when(ki != qi)
    def _():
        attend(None)

    @pl.when(last_ref[p_id] == 1)
    def _():
        for h in range(N_HEADS):
            o_ref[:, h * HEAD_W:(h + 1) * HEAD_W] = (acc_ref[h] / l_ref[h]).T.astype(o_ref.dtype)


def _mla_attn(qm, km, vm, *, batch, seq, tq=1024, tk=1024):
    assert tq == tk
    n = qm.shape[0]
    hw = N_HEADS * MLA_PAD
    nq, nk = seq // tq, seq // tk
    qi, ki, last = _causal_pairs(nq, tq, tk)
    grid_spec = pltpu.PrefetchScalarGridSpec(
        num_scalar_prefetch=3,
        grid=(batch, int(qi.shape[0])),
        in_specs=[pl.BlockSpec((tq, hw), lambda b, p_, qi_, ki_, la_: (b * nq + qi_[p_], 0)),
                  pl.BlockSpec((tk, hw), lambda b, p_, qi_, ki_, la_: (b * nk + ki_[p_], 0)),
                  pl.BlockSpec((tk, BRANCH_W), lambda b, p_, qi_, ki_, la_: (b * nk + ki_[p_], 0))],
        out_specs=pl.BlockSpec((tq, BRANCH_W), lambda b, p_, qi_, ki_, la_: (b * nq + qi_[p_], 0)),
        scratch_shapes=[pltpu.VMEM((N_HEADS, 1, tq), F32),
                        pltpu.VMEM((N_HEADS, 1, tq), F32),
                        pltpu.VMEM((N_HEADS, HEAD_W, tq), F32)],
    )
    return pl.pallas_call(
        functools.partial(_mla_attn_kernel, tq=tq, tk=tk),
        out_shape=jax.ShapeDtypeStruct((n, BRANCH_W), BF16),
        grid_spec=grid_spec,
        compiler_params=_cparams("parallel", "arbitrary"),
        name="mla_attn",
    )(qi, ki, last, qm, km, vm)


def _merge_kernel(ya_ref, yb_ref, yc_ref, yd_ref, g0_ref, g1_ref, g2_ref, g3_ref, wb_ref, o_ref):
    acc = None
    for n, (y_ref, g_ref) in enumerate(((ya_ref, g0_ref), (yb_ref, g1_ref), (yc_ref, g2_ref), (yd_ref, g3_ref))):
        up = jnp.dot(y_ref[...], wb_ref[n], preferred_element_type=F32)
        term = jax.nn.sigmoid(g_ref[...].astype(F32)) * up
        acc = term if acc is None else acc + term
    o_ref[...] = acc.astype(o_ref.dtype)


def _merge(ys, gates, wb, tm=1024, tn=1024):
    n = gates.shape[0]
    nj = D_MODEL // tn
    y_spec = pl.BlockSpec((tm, BRANCH_W), lambda i, j: (i, 0))
    g_specs = [pl.BlockSpec((tm, tn), functools.partial(lambda i, j, nn: (i, nn * nj + j), nn=nn)) for nn in range(4)]
    return pl.pallas_call(
        _merge_kernel,
        out_shape=jax.ShapeDtypeStruct((n, D_MODEL), BF16),
        grid=(n // tm, nj),
        in_specs=[y_spec] * 4 + g_specs + [pl.BlockSpec((4, BRANCH_W, tn), lambda i, j: (0, 0, j))],
        out_specs=pl.BlockSpec((tm, tn), lambda i, j: (i, j)),
        compiler_params=_cparams("parallel", "arbitrary"),
        name="merge",
    )(*ys, gates, gates, gates, gates, wb)


def _resid_matmul_kernel(m_ref, w_ref, x_ref, mod_ref, o_ref, *, gate_row):
    y = jnp.dot(m_ref[...], w_ref[...], preferred_element_type=F32)
    o_ref[...] = x_ref[...] + mod_ref[gate_row:gate_row + 1, :] * y


def _resid_matmul(m, w, x, mod, *, seq, gate_row, tm=2048, tn=512):
    n, d = x.shape
    kdim = m.shape[1]
    return pl.pallas_call(
        functools.partial(_resid_matmul_kernel, gate_row=gate_row),
        out_shape=jax.ShapeDtypeStruct((n, d), F32),
        grid=(n // tm, d // tn),
        in_specs=[pl.BlockSpec((tm, kdim), lambda i, j: (i, 0)),
                  pl.BlockSpec((kdim, tn), lambda i, j: (0, j)),
                  pl.BlockSpec((tm, tn), lambda i, j: (i, j)),
                  pl.BlockSpec((None, 6, tn), lambda i, j: (i * tm // seq, 0, j))],
        out_specs=pl.BlockSpec((tm, tn), lambda i, j: (i, j)),
        compiler_params=_cparams("parallel", "arbitrary"),
        name="out_proj",
    )(m, w, x, mod)


def _top1_rounds(problems, rounds):
    state = []
    for s, ids in problems:
        row = lax.broadcasted_iota(jnp.int32, s.shape, 0).astype(F32)
        state.append([s, ids, row, float(s.shape[0]), [], []])
    for _ in range(rounds):
        ms = [jnp.max(st[0], axis=0, keepdims=True) for st in state]
        poss = [jnp.min(jnp.where(st[0] == m, st[2], st[3]), axis=0, keepdims=True) for st, m in zip(state, ms)]
        for st, m, pos in zip(state, ms, poss):
            sel = st[2] == pos
            st[4].append(m)
            st[5].append(pos if st[1] is None else jnp.sum(jnp.where(sel, st[1], 0), axis=0, keepdims=True))
            st[0] = jnp.where(sel, NEG_INF, st[0])
    res = []
    for s, ids, row, nrow, vals, outs in state:
        out = jnp.concatenate(outs, axis=0)
        res.append((jnp.concatenate(vals, axis=0), out.astype(jnp.int32) if ids is None else out))
    return res


_CAND_ROWS = [(a, PEER_TOPK // (a + 1)) for a in range(PEER_TOPK)]
_CAND_PAD = -sum(nb for _, nb in _CAND_ROWS) % 8


def _route_kernel(q_ref, keys_ref, e_ref, g_ref, *, tt):
    subs = [slice(sub * 128, (sub + 1) * 128) for sub in range(tt // 128)]
    scores = []
    for rows in subs:
        for c in range(2):
            qc = q_ref[rows, c * 128:(c + 1) * 128]
            sc = lax.dot_general(keys_ref[c], qc, (((1,), (1,)), ((), ())), preferred_element_type=F32,
                                 precision=lax.Precision.HIGHEST)
            scores.append((sc, None))
    tops = _top1_rounds(scores, PEER_TOPK)
    cands = []
    for j in range(len(subs)):
        (s0, i0), (s1, i1) = tops[2 * j], tops[2 * j + 1]
        pad_s = [jnp.full((_CAND_PAD, 128), NEG_INF, F32)] if _CAND_PAD else []
        pad_i = [jnp.zeros((_CAND_PAD, 128), jnp.int32)] if _CAND_PAD else []
        cand_s = jnp.concatenate([s0[a:a + 1] + s1[:nb] for a, nb in _CAND_ROWS] + pad_s, axis=0)
        cand_i = jnp.concatenate([i0[a:a + 1] * PEER_KEYS + i1[:nb] for a, nb in _CAND_ROWS] + pad_i, axis=0)
        cands.append((cand_s, cand_i))
    for rows, (best_s, experts) in zip(subs, _top1_rounds(cands, PEER_TOPK)):
        ex = jnp.exp(best_s - jnp.max(best_s, axis=0, keepdims=True))
        e_ref[:, rows] = experts
        g_ref[:, rows] = ex / jnp.sum(ex, axis=0, keepdims=True)


def _peer_route(q, sub_keys, tt=1024):
    n = q.shape[0]
    e, g = pl.pallas_call(
        functools.partial(_route_kernel, tt=tt),
        out_shape=[jax.ShapeDtypeStruct((PEER_HEADS, PEER_TOPK, n), jnp.int32),
                   jax.ShapeDtypeStruct((PEER_HEADS, PEER_TOPK, n), F32)],
        grid=(n // tt, PEER_HEADS),
        in_specs=[pl.BlockSpec((tt, 256), lambda i, h: (i, h)),
                  pl.BlockSpec((None, 2, PEER_KEYS, 128), lambda i, h: (h, 0, 0, 0))],
        out_specs=[pl.BlockSpec((None, PEER_TOPK, tt), lambda i, h: (h, 0, i)),
                   pl.BlockSpec((None, PEER_TOPK, tt), lambda i, h: (h, 0, i))],
        compiler_params=_cparams("parallel", "parallel"),
        name="peer_route",
    )(q, sub_keys)
    return e.reshape(PEER_SEL, n), g.reshape(PEER_SEL, n)


PEER_TB = 128
PEER_NBUF = 4
D_CHUNKS = D_MODEL // 128


def _peer_apply_kernel(ex_ref, gt_ref, h_ref, x_ref, mod_ref, uv_ref, o_ref, buf_ref, h32_ref, sem):
    h32_ref[...] = h_ref[...].astype(F32)

    def issue(t, slot):
        for k in range(PEER_SEL):
            row = pl.multiple_of(ex_ref[t, k] * D_CHUNKS, D_CHUNKS)
            pltpu.make_async_copy(uv_ref.at[pl.ds(row, D_CHUNKS)], buf_ref.at[slot, :, k],
                                  sem.at[slot]).start(priority=k % 2)

    def wait(slot):
        for r in range(D_CHUNKS):
            pltpu.make_async_copy(uv_ref.at[pl.ds(0, PEER_SEL)], buf_ref.at[slot, r], sem.at[slot]).wait()

    def chunk(slot, r):
        return buf_ref[slot, r]

    lane = lax.broadcasted_iota(jnp.int32, (1, PEER_TB), 1)

    def compute(t, slot):
        acc = jnp.zeros((PEER_SEL, 128), F32)
        for r in range(D_CHUNKS):
            u = pltpu.bitcast(chunk(slot, r) & jnp.uint32(0xFFFF0000), F32)
            acc = acc + u * h32_ref[t, r:r + 1, :]
        s = jnp.sum(acc, axis=1, keepdims=True)
        gate = jnp.sum(jnp.where(lane == t, gt_ref[...], 0.0), axis=1, keepdims=True)
        act = _gelu(s) * gate
        outs = []
        for r in range(D_CHUNKS):
            v = pltpu.bitcast(chunk(slot, r) << 16, F32)
            outs.append(jnp.sum(v * act, axis=0, keepdims=True))
        out = jnp.concatenate(outs, axis=1)
        o_ref[pl.ds(t, 1), :] = x_ref[pl.ds(t, 1), :] + mod_ref[5:6, :] * out

    for slot in range(PEER_NBUF):
        issue(slot, slot)

    def body(i, carry):
        for slot in range(PEER_NBUF):
            t = i * PEER_NBUF + slot
            wait(slot)
            compute(t, slot)

            @pl.when(t + PEER_NBUF < PEER_TB)
            def _():
                issue(t + PEER_NBUF, slot)
        return carry

    lax.fori_loop(0, PEER_TB // PEER_NBUF, body, 0)


def _peer_apply(experts, gates_t, h2, x, mod, uv, *, seq):
    n, d = x.shape
    tb = PEER_TB
    return pl.pallas_call(
        _peer_apply_kernel,
        out_shape=jax.ShapeDtypeStruct((n, d), F32),
        grid=(n // tb,),
        in_specs=[pl.BlockSpec((tb, PEER_SEL), lambda i: (i, 0), memory_space=pltpu.SMEM),
                  pl.BlockSpec((PEER_SEL, tb), lambda i: (0, i)),
                  pl.BlockSpec((tb, D_CHUNKS, 128), lambda i: (i, 0, 0)),
                  pl.BlockSpec((tb, d), lambda i: (i, 0)),
                  pl.BlockSpec((None, 6, d), lambda i: (i * tb // seq, 0, 0)),
                  pl.BlockSpec(memory_space=pl.ANY)],
        out_specs=pl.BlockSpec((tb, d), lambda i: (i, 0)),
        scratch_shapes=[pltpu.VMEM((PEER_NBUF, D_CHUNKS, PEER_SEL, 128), jnp.uint32),
                        pltpu.VMEM((tb, D_CHUNKS, 128), F32),
                        pltpu.SemaphoreType.DMA((PEER_NBUF,))],
        compiler_params=_cparams("arbitrary"),
        name="peer_apply",
    )(experts, gates_t, h2.reshape(n, D_CHUNKS, 128), x, mod, uv.reshape(-1, 128))


def _pack_uv(u_tab, v_tab):
    ub = lax.bitcast_convert_type(u_tab.astype(BF16), jnp.uint16).astype(jnp.uint32)
    vb = lax.bitcast_convert_type(v_tab.astype(BF16), jnp.uint16).astype(jnp.uint32)
    return (ub << 16) | vb


def _main_proj_weights(w_in):
    d = w_in.shape[0]
    main = jnp.concatenate([w_in[:, :4608], w_in[:, 4616:5448], w_in[:, 4608:4616],
                            jnp.zeros((d, P_WIDTH - 5448), w_in.dtype)], axis=1)
    return main.astype(BF16), w_in[:, 5448:].astype(BF16)


def _mla_weights(w_uq, w_ukv):
    qr = w_uq.shape[0]
    kvr = w_ukv.shape[0]
    wq = jnp.pad(w_uq.reshape(qr, N_HEADS, MLA_QK), ((0, 0), (0, 0), (0, MLA_PAD - MLA_QK))).reshape(qr, -1)
    kv = w_ukv.reshape(kvr, N_HEADS, MLA_NOPE + HEAD_W)
    k_nope = jnp.pad(kv[:, :, :MLA_NOPE], ((0, 0), (0, 0), (MLA_ROPE, MLA_PAD - MLA_QK))).reshape(kvr, -1)
    eye = jnp.pad(jnp.eye(MLA_ROPE, dtype=w_ukv.dtype), ((0, 128 - MLA_ROPE), (0, MLA_PAD - MLA_ROPE)))
    wk = jnp.concatenate([k_nope, jnp.tile(eye, (1, N_HEADS))], axis=0)
    wv = kv[:, :, MLA_NOPE:].reshape(kvr, -1)
    return wq.astype(BF16), wk.astype(BF16), wv.astype(BF16)


def kernel(x, c, positions, norm1_g, norm2_g, w_mod, b_mod, w_in, da_q_norm, da_k_norm, da_lambda, da_out_norm, sg_v_norm, sg_w, sg_b, ml_conv_w, ml_conv_b, ml_gate_b, ml_out_norm, mla_q_norm, mla_kv_norm, mla_w_uq, mla_w_ukv, mla_qk_norm, w_branch, w_out, peer_w_q, peer_sub_keys, peer_u, peer_v):
    batch, seq, d = x.shape
    depth = w_mod.shape[0]
    n = batch * seq
    xf = x.reshape(n, d)

    mod_all = _modulation(c, w_mod, b_mod)
    pos_col = positions.reshape(n, 1).astype(F32)
    cos_da, sin_da = _rope_tables(pos_col, _lane_freq_sign(DA_QK, DA_ROT))
    cos_ml, sin_ml = _rope_tables(pos_col, _lane_freq_sign(128, MLA_ROPE))

    for l in range(depth):
        mod = mod_all[l]
        lambda_init = 0.8 - 0.6 * math.exp(-0.3 * l)
        w_main, w_gates = _main_proj_weights(w_in[l])
        p = _norm_matmul(xf, norm1_g[l], mod, w_main, seq=seq, sc_row=1, sh_row=0, out_dtype=F32, tm=1024, tn=P_WIDTH // 4,
                         name="in_proj")
        gates = _norm_matmul(xf, norm1_g[l], mod, w_gates, seq=seq, sc_row=1, sh_row=0, out_dtype=BF16,
                             tm=1024, tn=2048, name="gate_proj")

        qn, kn = _da_prep(p, cos_da, sin_da, da_q_norm[l], da_k_norm[l])
        ya = _da_attn(qn, kn, p, da_lambda[l], da_out_norm[l], batch=batch, seq=seq, lambda_init=lambda_init)

        yb = _spatial_gating(p, sg_v_norm[l], sg_w[l], sg_b[l])

        qk = _ml_conv(p, ml_conv_w[l], ml_conv_b[l], seq=seq)
        yc = _mlstm(qk, p, ml_gate_b[l], ml_out_norm[l], batch=batch, seq=seq)

        wq, wk, wv = _mla_weights(mla_w_uq[l], mla_w_ukv[l])
        qk_g = jnp.pad(mla_qk_norm[l], ((0, 0), (0, MLA_PAD - MLA_QK)))
        qm, km, vm = _mla_prep(p, cos_ml, sin_ml, mla_q_norm[l], mla_kv_norm[l], wq, wk, wv, qk_g)
        yd = _mla_attn(qm, km, vm, batch=batch, seq=seq)

        mixed = _merge((ya, yb, yc, yd), gates, w_branch[l].astype(BF16))
        xf = _resid_matmul(mixed, w_out[l].astype(BF16), xf, mod, seq=seq, gate_row=2)

        q_peer, h2 = _norm_matmul(xf, norm2_g[l], mod, peer_w_q[l].astype(BF16), seq=seq, sc_row=4, sh_row=3,
                                  out_dtype=F32, emit_h=True, tm=1024, tn=1024, name="peer_query")
        experts_t, gates_t = _peer_route(q_peer, peer_sub_keys[l])
        uv = _pack_uv(peer_u[l], peer_v[l])
        xf = _peer_apply(experts_t.T, gates_t, h2, xf, mod, uv, seq=seq)

    return xf.reshape(batch, seq, d)
```

```python
import functools
import math

import numpy as np
import jax
import jax.numpy as jnp
from jax import lax
from jax.experimental import pallas as pl
from jax.experimental.pallas import tpu as pltpu

F32 = jnp.float32
BF16 = jnp.bfloat16

D_MODEL = 2048
CHUNK = 64
ROPE_THETA = 500000.0
RMS_EPS = 1e-6
N_HEADS = 4
HEAD_W = 128
BRANCH_W = N_HEADS * HEAD_W
DA_QK = 64
DA_ROT = 16
ML_CONV = 4
MLA_ROPE = 64
MLA_NOPE = 128
MLA_QK = MLA_ROPE + MLA_NOPE
MLA_PAD = 256
MLA_Q_RANK = 512
MLA_KV_RANK = 256
PEER_HEADS = 8
PEER_KEYS = 128
PEER_TOPK = 16
PEER_SEL = PEER_HEADS * PEER_TOPK

P_WIDTH = 11 * 512
COL_DAQ, COL_DAK, COL_DAV, COL_SGU, COL_SGV, COL_MLQK, COL_MLV, COL_MLO, COL_CQ, COL_MISC = 0, 1, 2, 3, 4, 5, 7, 8, 9, 10
MISC_KR = 256
MISC_GATE = 320

VMEM_LIMIT = 56 * 1024 * 1024
NEG_INF = float("-inf")
LOG2_E = math.log2(math.e)


def _cparams(*sem):
    return pltpu.CompilerParams(dimension_semantics=sem, vmem_limit_bytes=VMEM_LIMIT)


def _gelu(x):
    return 0.5 * x * (1.0 + lax.erf(x * (1.0 / math.sqrt(2.0))))


def _dot_nt(a, b):
    return lax.dot_general(a, b, (((1,), (1,)), ((), ())), preferred_element_type=F32)


def _mod_kernel(c_ref, w_ref, b_ref, o_ref):
    c = c_ref[...]
    cond = c * jax.nn.sigmoid(c)
    o_ref[0] = jnp.dot(cond, w_ref[0], preferred_element_type=F32,
                       precision=lax.Precision.HIGHEST) + b_ref[0]


def _modulation(c, w_mod, b_mod):
    depth, d, w6 = w_mod.shape
    b = c.shape[0]
    rows = 8
    cp = jnp.zeros((rows, d), F32).at[:b].set(c)
    tn = 1024
    out = pl.pallas_call(
        _mod_kernel,
        out_shape=jax.ShapeDtypeStruct((depth, rows, w6), F32),
        grid=(depth, w6 // tn),
        in_specs=[pl.BlockSpec((rows, d), lambda l, j: (0, 0)),
                  pl.BlockSpec((1, d, tn), lambda l, j: (l, 0, j)),
                  pl.BlockSpec((1, 1, tn), lambda l, j: (l, 0, j))],
        out_specs=pl.BlockSpec((1, rows, tn), lambda l, j: (l, 0, j)),
        compiler_params=_cparams("parallel", "parallel"),
        name="modulation",
    )(cp, w_mod, b_mod.reshape(depth, 1, w6))
    return out[:, :b].reshape(depth, b, 6, d)


def _norm_mod_kernel(x_ref, g_ref, mod_ref, h_ref, *, sc_row, sh_row):
    x = x_ref[...]
    ms = jnp.mean(x * x, axis=-1, keepdims=True)
    y = x * lax.rsqrt(ms + RMS_EPS) * g_ref[...]
    h_ref[...] = (y * (1.0 + mod_ref[sc_row:sc_row + 1, :]) + mod_ref[sh_row:sh_row + 1, :]).astype(h_ref.dtype)


def _norm_mod(x, g, mod, *, seq, sc_row, sh_row, tm=1024):
    n, d = x.shape
    return pl.pallas_call(
        functools.partial(_norm_mod_kernel, sc_row=sc_row, sh_row=sh_row),
        out_shape=jax.ShapeDtypeStruct((n, d), BF16),
        grid=(n // tm,),
        in_specs=[pl.BlockSpec((tm, d), lambda i: (i, 0)),
                  pl.BlockSpec((1, d), lambda i: (0, 0)),
                  pl.BlockSpec((None, 6, d), lambda i: (i * tm // seq, 0, 0))],
        out_specs=pl.BlockSpec((tm, d), lambda i: (i, 0)),
        compiler_params=_cparams("parallel"),
        name="norm_mod",
    )(x, g.reshape(1, d), mod)


def _matmul_kernel(h_ref, w_ref, o_ref):
    o_ref[...] = jnp.dot(h_ref[...], w_ref[...], preferred_element_type=F32).astype(o_ref.dtype)


def _matmul(h, w, *, out_dtype, tm, tn, name):
    n, d = h.shape
    wn = w.shape[1]
    return pl.pallas_call(
        _matmul_kernel,
        out_shape=jax.ShapeDtypeStruct((n, wn), out_dtype),
        grid=(n // tm, wn // tn),
        in_specs=[pl.BlockSpec((tm, d), lambda i, j: (i, 0)),
                  pl.BlockSpec((d, tn), lambda i, j: (0, j))],
        out_specs=pl.BlockSpec((tm, tn), lambda i, j: (i, j)),
        compiler_params=_cparams("parallel", "arbitrary"),
        name=name,
    )(h, w)


def _rope_table_kernel(pos_ref, f_ref, cos_ref, sin_ref):
    ang = pos_ref[...] * f_ref[0:1, :]
    cos_ref[...] = jnp.cos(ang)
    sin_ref[...] = jnp.sin(ang) * f_ref[1:2, :]


def _rope_tables(pos_col, freq_sign, tm=1024):
    n = pos_col.shape[0]
    return pl.pallas_call(
        _rope_table_kernel,
        out_shape=[jax.ShapeDtypeStruct((n, 128), F32)] * 2,
        grid=(n // tm,),
        in_specs=[pl.BlockSpec((tm, 1), lambda i: (i, 0)),
                  pl.BlockSpec((2, 128), lambda i: (0, 0))],
        out_specs=[pl.BlockSpec((tm, 128), lambda i: (i, 0))] * 2,
        compiler_params=_cparams("parallel"),
        name="rope_tables",
    )(pos_col, freq_sign)


def _lane_freq_sign(group, rot):
    half = rot // 2
    freq = ROPE_THETA ** (-jnp.arange(half, dtype=F32) / half)
    lane = np.arange(128) % group
    idx = np.where(lane < rot, lane % half, 0)
    f = jnp.where(jnp.asarray(lane < rot), freq[idx], 0.0)
    sign = np.where(lane < half, -1.0, np.where(lane < rot, 1.0, 0.0)).astype(np.float32)
    return jnp.stack([f, jnp.asarray(sign)], axis=0)


def _rope_apply(x, cos, sin, group, rot):
    half = rot // 2
    lane = lax.broadcasted_iota(jnp.int32, (1, 128), 1) % group
    partner = jnp.where(lane < half, pltpu.roll(x, 128 - half, 1), pltpu.roll(x, half, 1))
    return x * cos + partner * sin


def _da_prep_kernel(q_ref, k_ref, cos_ref, sin_ref, qg_ref, kg_ref, qo_ref, ko_ref):
    cos = cos_ref[...]
    sin = sin_ref[...]
    lo = lax.broadcasted_iota(jnp.int32, (1, 128), 1) < DA_QK

    def prep(ref, g_ref, o_ref, scale):
        for h in range(N_HEADS):
            x = ref[:, h * HEAD_W:(h + 1) * HEAD_W]
            sq = x * x
            s_lo = jnp.sum(jnp.where(lo, sq, 0.0), axis=-1, keepdims=True)
            s_hi = jnp.sum(jnp.where(lo, 0.0, sq), axis=-1, keepdims=True)
            ms = jnp.where(lo, s_lo, s_hi) * (1.0 / DA_QK)
            y = x * lax.rsqrt(ms + RMS_EPS) * g_ref[...]
            y = _rope_apply(y, cos, sin, DA_QK, DA_ROT)
            o_ref[:, h * HEAD_W:(h + 1) * HEAD_W] = (y * scale).astype(BF16)

    prep(q_ref, qg_ref, qo_ref, DA_QK ** -0.5 * LOG2_E)
    prep(k_ref, kg_ref, ko_ref, 1.0)


def _da_prep(p, cos, sin, q_g, k_g, tm=1024):
    n = p.shape[0]
    return pl.pallas_call(
        _da_prep_kernel,
        out_shape=[jax.ShapeDtypeStruct((n, BRANCH_W), BF16)] * 2,
        grid=(n // tm,),
        in_specs=[pl.BlockSpec((tm, BRANCH_W), lambda i: (i, COL_DAQ)),
                  pl.BlockSpec((tm, BRANCH_W), lambda i: (i, COL_DAK)),
                  pl.BlockSpec((tm, 128), lambda i: (i, 0)),
                  pl.BlockSpec((tm, 128), lambda i: (i, 0)),
                  pl.BlockSpec((1, 128), lambda i: (0, 0)),
                  pl.BlockSpec((1, 128), lambda i: (0, 0))],
        out_specs=[pl.BlockSpec((tm, BRANCH_W), lambda i: (i, 0))] * 2,
        compiler_params=_cparams("parallel"),
        name="da_prep",
    )(p, p, cos, sin, q_g.reshape(1, 128), k_g.reshape(1, 128))


def _causal_pairs(nq, tq, tk):
    qi, ki, last = [], [], []
    for q in range(nq):
        nk = -(-((q + 1) * tq) // tk)
        for k in range(nk):
            qi.append(q)
            ki.append(k)
            last.append(int(k == nk - 1))
    return (jnp.asarray(qi, jnp.int32), jnp.asarray(ki, jnp.int32), jnp.asarray(last, jnp.int32))


def _attend(score_fns, values, mask, m_ref, l_ref, acc_ref):
    n = len(score_fns)
    scores, probs = {}, {}
    for i in range(n + 2):
        if i < n:
            s = score_fns[i]()
            scores[i] = s if mask is None else jnp.where(mask, s, NEG_INF)
        j = i - 1
        if 0 <= j < n:
            s = scores.pop(j)
            m_prev = m_ref[j]
            m_new = jnp.maximum(m_prev, jnp.max(s, axis=0, keepdims=True))
            alpha = jnp.exp2(m_prev - m_new)
            p = jnp.exp2(s - m_new)
            l_ref[j] = alpha * l_ref[j] + jnp.sum(p, axis=0, keepdims=True)
            m_ref[j] = m_new
            probs[j] = (p.astype(BF16), alpha)
        j = i - 2
        if 0 <= j < n:
            p, alpha = probs.pop(j)
            pv = lax.dot_general(values[j], p, (((0,), (0,)), ((), ())), preferred_element_type=F32)
            acc_ref[j] = alpha * acc_ref[j] + pv


def _chunk_mask_t(qi, ki, tq, tk):
    key = ki * tk + lax.broadcasted_iota(jnp.int32, (tk, 1), 0)
    qry = qi * tq + lax.broadcasted_iota(jnp.int32, (1, tq), 1)
    return (key // CHUNK) <= (qry // CHUNK)


def _init_softmax_state(m_ref, l_ref, acc_ref):
    m_ref[...] = jnp.full(m_ref.shape, NEG_INF, F32)
    l_ref[...] = jnp.zeros(l_ref.shape, F32)
    acc_ref[...] = jnp.zeros(acc_ref.shape, F32)


def _da_attn_kernel(qi_ref, ki_ref, last_ref, q_ref, k_ref, v_ref, lam_ref, og_ref, o_ref,
                    m_ref, l_ref, acc_ref, *, lambda_init, tq, tk):
    p_id = pl.program_id(1)
    qi = qi_ref[p_id]
    ki = ki_ref[p_id]

    @pl.when(ki == 0)
    def _():
        _init_softmax_state(m_ref, l_ref, acc_ref)

    lo = lax.broadcasted_iota(jnp.int32, (1, HEAD_W), 1) < DA_QK

    def attend(mask):
        score_fns, values = [], []
        for h in range(N_HEADS):
            sl = slice(h * HEAD_W, (h + 1) * HEAD_W)
            v = v_ref[:, sl].astype(BF16)
            for c in range(2):
                def score(sl=sl, c=c):
                    q = q_ref[:, sl]
                    zero = jnp.zeros_like(q)
                    return _dot_nt(k_ref[:, sl], jnp.where(lo, q, zero) if c == 0 else jnp.where(lo, zero, q))
                score_fns.append(score)
                values.append(v)
        _attend(score_fns, values, mask, m_ref, l_ref, acc_ref)

    @pl.when(ki == qi)
    def _():
        attend(_chunk_mask_t(qi, ki, tq, tk))

    @pl.when(ki != qi)
    def _():
        attend(None)

    @pl.when(last_ref[p_id] == 1)
    def _():
        lf = lam_ref[...]
        lam = (jnp.exp(jnp.sum(lf[0:1] * lf[1:2], axis=-1, keepdims=True))
               - jnp.exp(jnp.sum(lf[2:3] * lf[3:4], axis=-1, keepdims=True)) + lambda_init)
        for h in range(N_HEADS):
            o = acc_ref[2 * h] / l_ref[2 * h] - lam * (acc_ref[2 * h + 1] / l_ref[2 * h + 1])
            ms = jnp.mean(o * o, axis=0, keepdims=True)
            y = o * lax.rsqrt(ms + RMS_EPS) * og_ref[...] * (1.0 - lambda_init)
            o_ref[:, h * HEAD_W:(h + 1) * HEAD_W] = y.T.astype(o_ref.dtype)


def _da_attn(qn, kn, p, lam_vecs, out_g, *, batch, seq, lambda_init, tq=1024, tk=1024):
    assert tq == tk
    n = qn.shape[0]
    nq, nk = seq // tq, seq // tk
    qi, ki, last = _causal_pairs(nq, tq, tk)
    grid_spec = pltpu.PrefetchScalarGridSpec(
        num_scalar_prefetch=3,
        grid=(batch, int(qi.shape[0])),
        in_specs=[pl.BlockSpec((tq, BRANCH_W), lambda b, p_, qi_, ki_, la_: (b * nq + qi_[p_], 0)),
                  pl.BlockSpec((tk, BRANCH_W), lambda b, p_, qi_, ki_, la_: (b * nk + ki_[p_], 0)),
                  pl.BlockSpec((tk, BRANCH_W), lambda b, p_, qi_, ki_, la_: (b * nk + ki_[p_], COL_DAV)),
                  pl.BlockSpec((4, DA_QK), lambda b, p_, qi_, ki_, la_: (0, 0)),
                  pl.BlockSpec((HEAD_W, 1), lambda b, p_, qi_, ki_, la_: (0, 0))],
        out_specs=pl.BlockSpec((tq, BRANCH_W), lambda b, p_, qi_, ki_, la_: (b * nq + qi_[p_], 0)),
        scratch_shapes=[pltpu.VMEM((2 * N_HEADS, 1, tq), F32),
                        pltpu.VMEM((2 * N_HEADS, 1, tq), F32),
                        pltpu.VMEM((2 * N_HEADS, HEAD_W, tq), F32)],
    )
    return pl.pallas_call(
        functools.partial(_da_attn_kernel, lambda_init=lambda_init, tq=tq, tk=tk),
        out_shape=jax.ShapeDtypeStruct((n, BRANCH_W), BF16),
        grid_spec=grid_spec,
        compiler_params=_cparams("parallel", "arbitrary"),
        name="da_attn",
    )(qi, ki, last, qn, kn, p, lam_vecs, out_g.reshape(HEAD_W, 1))


def _sg_kernel(u_ref, v_ref, vg_ref, w_ref, b_ref, o_ref, *, tm):
    idx_i = lax.broadcasted_iota(jnp.int32, (128, 128), 0)
    idx_j = lax.broadcasted_iota(jnp.int32, (128, 128), 1)
    wmask = (idx_j // CHUNK) <= (idx_i // CHUNK)
    for g in range(N_HEADS):
        sl = slice(g * HEAD_W, (g + 1) * HEAD_W)
        u = _gelu(u_ref[:, sl])
        v = _gelu(v_ref[:, sl])
        ms = jnp.mean(v * v, axis=-1, keepdims=True)
        vn = (v * lax.rsqrt(ms + RMS_EPS) * vg_ref[g:g + 1, :]).astype(BF16)
        w = jnp.where(wmask, w_ref[g], 0.0).astype(BF16)
        bias = b_ref[:, g:g + 1]
        for nb in range(tm // 128):
            rows = slice(nb * 128, (nb + 1) * 128)
            sv = jnp.dot(w, vn[rows], preferred_element_type=F32) + bias
            o_ref[rows, sl] = (u[rows] * sv).astype(o_ref.dtype)


def _spatial_gating(p, v_g, w_s, b_s, tm=1024):
    n = p.shape[0]
    return pl.pallas_call(
        functools.partial(_sg_kernel, tm=tm),
        out_shape=jax.ShapeDtypeStruct((n, BRANCH_W), BF16),
        grid=(n // tm,),
        in_specs=[pl.BlockSpec((tm, BRANCH_W), lambda i: (i, COL_SGU)),
                  pl.BlockSpec((tm, BRANCH_W), lambda i: (i, COL_SGV)),
                  pl.BlockSpec((N_HEADS, 128), lambda i: (0, 0)),
                  pl.BlockSpec((N_HEADS, 128, 128), lambda i: (0, 0, 0)),
                  pl.BlockSpec((128, N_HEADS), lambda i: (0, 0))],
        out_specs=pl.BlockSpec((tm, BRANCH_W), lambda i: (i, 0)),
        compiler_params=_cparams("parallel"),
        name="spatial_gating",
    )(p, p, v_g, w_s, b_s.T)


def _ml_conv_kernel(x_ref, halo_ref, w_ref, b_ref, o_ref, *, tm, seq):
    i = pl.program_id(0)
    first = (i * tm) % seq == 0
    halo = jnp.where(first, 0.0, halo_ref[...])
    xc = jnp.concatenate([halo, x_ref[...]], axis=0)
    acc = jnp.zeros((tm, x_ref.shape[1]), F32) + b_ref[...]
    for j in range(ML_CONV):
        shift = ML_CONV - 1 - j
        xs = xc if shift == 0 else pltpu.roll(xc, shift, 0)
        acc = acc + xs[8:8 + tm] * w_ref[j:j + 1, :]
    o_ref[...] = acc * jax.nn.sigmoid(acc)


def _ml_conv(p, conv_w, conv_b, *, seq, tm=2048):
    n = p.shape[0]
    width = 2 * BRANCH_W
    cb = width // BRANCH_W
    return pl.pallas_call(
        functools.partial(_ml_conv_kernel, tm=tm, seq=seq),
        out_shape=jax.ShapeDtypeStruct((n, width), F32),
        grid=(n // tm, cb),
        in_specs=[pl.BlockSpec((tm, BRANCH_W), lambda i, j: (i, COL_MLQK + j)),
                  pl.BlockSpec((8, BRANCH_W), lambda i, j: (jnp.maximum(i * (tm // 8) - 1, 0), COL_MLQK + j)),
                  pl.BlockSpec((ML_CONV, BRANCH_W), lambda i, j: (0, j)),
                  pl.BlockSpec((1, BRANCH_W), lambda i, j: (0, j))],
        out_specs=pl.BlockSpec((tm, BRANCH_W), lambda i, j: (i, j)),
        compiler_params=_cparams("parallel", "parallel"),
        name="ml_conv",
    )(p, p, conv_w, conv_b.reshape(1, width))


def _log_sigmoid(x):
    return jnp.minimum(x, 0.0) - jnp.log1p(jnp.exp(-jnp.abs(x)))


def _mlstm_kernel(q_ref, k_ref, v_ref, o_ref, gc_ref, gr_ref, bc_ref, br_ref, og_ref, y_ref,
                  c_ref, n_ref, m_ref, *, batch):
    @pl.when(pl.program_id(0) == 0)
    def _():
        c_ref[...] = jnp.zeros(c_ref.shape, F32)
        n_ref[...] = jnp.zeros(n_ref.shape, F32)
        m_ref[...] = jnp.zeros(m_ref.shape, F32)

    L = CHUNK
    it = lax.broadcasted_iota(jnp.int32, (L, L), 0)
    is_ = lax.broadcasted_iota(jnp.int32, (L, L), 1)
    tril = is_ <= it
    triu = it <= is_
    chains = [(b, h) for b in range(batch) for h in range(N_HEADS)]
    ins = []
    for b, h in chains:
        st = b * N_HEADS + h
        sl = slice(h * HEAD_W, (h + 1) * HEAD_W)
        ins.append((q_ref[b, :, sl], k_ref[b, :, sl], v_ref[b, :, sl], o_ref[b, :, sl],
                    gc_ref[b, :, MISC_GATE:MISC_GATE + 2 * N_HEADS] + bc_ref[...], gr_ref[b] + br_ref[...],
                    c_ref[st], n_ref[st], m_ref[st][:, 0:1]))
    each = lambda f, *cols: [f(*xs) for xs in zip(*cols)]
    hs = [h for _, h in chains]
    q32, k, v, o, gcol, grow, c_prev, n_prev, m_prev = (list(col) for col in zip(*ins))
    q = each(lambda x: x.astype(BF16), q32)
    k = each(lambda x: x * (HEAD_W ** -0.5), k)
    kb = each(lambda x: x.astype(BF16), k)
    vb = each(lambda x: x.astype(BF16), v)
    li_row = each(lambda gr, h: gr[h:h + 1, :], grow, hs)
    lf_row = each(lambda gr, h: _log_sigmoid(gr[N_HEADS + h:N_HEADS + h + 1, :]), grow, hs)
    li_col = each(lambda gc, h: gc[:, h:h + 1], gcol, hs)
    lf_col = each(lambda gc, h: _log_sigmoid(gc[:, N_HEADS + h:N_HEADS + h + 1]), gcol, hs)
    bcum_col = each(lambda x: jnp.sum(jnp.where(tril, x, 0.0), axis=1, keepdims=True), lf_row)
    bcum_row = each(lambda x: jnp.sum(jnp.where(triu, x, 0.0), axis=0, keepdims=True), lf_col)
    g = each(lambda x: jnp.sum(x, axis=1, keepdims=True), lf_row)
    dmat = each(lambda bc, br, li: jnp.where(tril, bc - br + li, NEG_INF), bcum_col, bcum_row, li_row)
    m_inter = each(lambda bc, m: bc + m, bcum_col, m_prev)
    m_t = each(lambda mi, d: jnp.maximum(mi, jnp.max(d, axis=1, keepdims=True)), m_inter, dmat)
    a_inter = each(lambda mi, mt: jnp.exp(mi - mt), m_inter, m_t)
    qk = each(_dot_nt, q, kb)
    sm = each(lambda d, mt, s_: jnp.exp(d - mt) * s_, dmat, m_t, qk)
    qc = each(lambda q_, c: jnp.dot(q_, c.astype(BF16), preferred_element_type=F32), q, c_prev)
    sv = each(lambda s_, v_: jnp.dot(s_.astype(BF16), v_, preferred_element_type=F32), sm, vb)
    num = each(lambda ai, x, y_: ai * x + y_, a_inter, qc, sv)
    qn = each(lambda q_, n_: jnp.sum(q_ * n_, axis=1, keepdims=True), q32, n_prev)
    den = each(lambda ai, x, s_: ai * x + jnp.sum(s_, axis=1, keepdims=True), a_inter, qn, sm)
    h_t = each(lambda nu, de, mt: nu / jnp.maximum(jnp.abs(de), jnp.exp(-mt)), num, den, m_t)
    dec_row = each(lambda g_, br, li: g_ - br + li, g, bcum_row, li_row)
    dec_col = each(lambda g_, bc, li: g_ - bc + li, g, bcum_col, li_col)
    m_new = each(lambda g_, m, d: jnp.maximum(g_ + m, jnp.max(d, axis=1, keepdims=True)), g, m_prev, dec_row)
    w_col = each(lambda d, m: jnp.exp(d - m), dec_col, m_new)
    a = each(lambda g_, m, mn: jnp.exp(g_ + m - mn), g, m_prev, m_new)
    kw = each(lambda w, k_: w * k_, w_col, k)
    kv = each(lambda kw_, v_: lax.dot_general(kw_.astype(BF16), v_, (((0,), (0,)), ((), ())),
                                              preferred_element_type=F32), kw, vb)
    c_new = each(lambda a_, c, x: a_ * c + x, a, c_prev, kv)
    n_new = each(lambda a_, n_, kw_: a_ * n_ + jnp.sum(kw_, axis=0, keepdims=True), a, n_prev, kw)
    y = each(lambda o_, h_: jax.nn.sigmoid(o_) * h_, o, h_t)
    ms = each(lambda y_: jnp.mean(y_ * y_, axis=-1, keepdims=True), y)
    y = each(lambda y_, ms_, h: (y_ * lax.rsqrt(ms_ + RMS_EPS) * og_ref[h:h + 1, :]).astype(y_ref.dtype), y, ms, hs)
    for (b, h), c_, n_, m_, y_ in zip(chains, c_new, n_new, m_new, y):
        st = b * N_HEADS + h
        c_ref[st] = c_
        n_ref[st] = n_
        m_ref[st] = jnp.broadcast_to(m_, (1, HEAD_W))
        y_ref[b, :, h * HEAD_W:(h + 1) * HEAD_W] = y_


def _mlstm(qk, p, gate_b, out_g, *, batch, seq):
    n = p.shape[0]
    nc = seq // CHUNK
    g8 = p[:, COL_MISC * 512 + MISC_GATE:COL_MISC * 512 + MISC_GATE + 2 * N_HEADS]
    grow = jnp.swapaxes(g8.reshape(batch, nc, CHUNK, 2 * N_HEADS), 2, 3)
    bias = gate_b.reshape(1, 2 * N_HEADS)
    qk3 = qk.reshape(batch, seq, qk.shape[1])
    p3 = p.reshape(batch, seq, p.shape[1])
    blk = lambda col: pl.BlockSpec((batch, CHUNK, BRANCH_W), functools.partial(lambda c, col_: (0, c, col_), col_=col))
    y = pl.pallas_call(
        functools.partial(_mlstm_kernel, batch=batch),
        out_shape=jax.ShapeDtypeStruct((batch, seq, BRANCH_W), BF16),
        grid=(nc,),
        in_specs=[blk(0), blk(1), blk(COL_MLV), blk(COL_MLO), blk(COL_MISC),
                  pl.BlockSpec((batch, None, 2 * N_HEADS, CHUNK), lambda c: (0, c, 0, 0)),
                  pl.BlockSpec((1, 2 * N_HEADS), lambda c: (0, 0)),
                  pl.BlockSpec((2 * N_HEADS, 1), lambda c: (0, 0)),
                  pl.BlockSpec((N_HEADS, HEAD_W), lambda c: (0, 0))],
        out_specs=pl.BlockSpec((batch, CHUNK, BRANCH_W), lambda c: (0, c, 0)),
        scratch_shapes=[pltpu.VMEM((batch * N_HEADS, HEAD_W, HEAD_W), F32),
                        pltpu.VMEM((batch * N_HEADS, 1, HEAD_W), F32),
                        pltpu.VMEM((batch * N_HEADS, 1, HEAD_W), F32)],
        compiler_params=_cparams("arbitrary"),
        name="mlstm",
    )(qk3, qk3, p3, p3, p3, grow, bias, bias.reshape(2 * N_HEADS, 1), out_g)
    return y.reshape(n, BRANCH_W)


def _mla_prep_kernel(cq_ref, misc_ref, cos_ref, sin_ref, qa_ref, kva_ref, wq_ref, wk_ref, wv_ref, qkg_ref,
                     qo_ref, ko_ref, vo_ref):
    def rms(x, g):
        ms = jnp.mean(x * x, axis=-1, keepdims=True)
        return x * lax.rsqrt(ms + RMS_EPS) * g

    cqn = rms(cq_ref[...], qa_ref[...]).astype(BF16)
    ckvn = rms(misc_ref[:, :MLA_KV_RANK], kva_ref[...]).astype(BF16)
    lane = lax.broadcasted_iota(jnp.int32, (1, 128), 1)
    kr = jnp.where(lane < MLA_ROPE, misc_ref[:, MISC_KR:MISC_KR + 128], 0.0).astype(BF16)
    q_lin = jnp.dot(cqn, wq_ref[...], preferred_element_type=F32)
    k_lin = jnp.dot(jnp.concatenate([ckvn, kr], axis=1), wk_ref[...], preferred_element_type=F32)
    vo_ref[...] = jnp.dot(ckvn, wv_ref[...], preferred_element_type=F32).astype(vo_ref.dtype)
    cos = cos_ref[...]
    sin = sin_ref[...]

    def finish(x_lin, g_row, o_ref, scale):
        for h in range(N_HEADS):
            x = x_lin[:, h * MLA_PAD:(h + 1) * MLA_PAD]
            ms = jnp.sum(x * x, axis=-1, keepdims=True) * (1.0 / MLA_QK)
            y = x * lax.rsqrt(ms + RMS_EPS) * qkg_ref[g_row:g_row + 1, :]
            y0 = _rope_apply(y[:, :128], cos, sin, 128, MLA_ROPE)
            o_ref[:, h * MLA_PAD:h * MLA_PAD + 128] = (y0 * scale).astype(o_ref.dtype)
            o_ref[:, h * MLA_PAD + 128:(h + 1) * MLA_PAD] = (y[:, 128:] * scale).astype(o_ref.dtype)

    finish(q_lin, 0, qo_ref, MLA_QK ** -0.5 * LOG2_E)
    finish(k_lin, 1, ko_ref, 1.0)


def _mla_prep(p, cos, sin, q_a_g, kv_a_g, wq, wk, wv, qk_g, tm=1024):
    n = p.shape[0]
    hw = N_HEADS * MLA_PAD
    full = lambda shape: pl.BlockSpec(shape, lambda i: (0,) * len(shape))
    return pl.pallas_call(
        _mla_prep_kernel,
        out_shape=[jax.ShapeDtypeStruct((n, hw), BF16), jax.ShapeDtypeStruct((n, hw), BF16),
                   jax.ShapeDtypeStruct((n, BRANCH_W), BF16)],
        grid=(n // tm,),
        in_specs=[pl.BlockSpec((tm, 512), lambda i: (i, COL_CQ)),
                  pl.BlockSpec((tm, 512), lambda i: (i, COL_MISC)),
                  pl.BlockSpec((tm, 128), lambda i: (i, 0)),
                  pl.BlockSpec((tm, 128), lambda i: (i, 0)),
                  full((1, MLA_Q_RANK)), full((1, MLA_KV_RANK)),
                  full(wq.shape), full(wk.shape), full(wv.shape), full((2, MLA_PAD))],
        out_specs=[pl.BlockSpec((tm, hw), lambda i: (i, 0)), pl.BlockSpec((tm, hw), lambda i: (i, 0)),
                   pl.BlockSpec((tm, BRANCH_W), lambda i: (i, 0))],
        compiler_params=_cparams("parallel"),
        name="mla_prep",
    )(p, p, cos, sin, q_a_g.reshape(1, -1), kv_a_g.reshape(1, -1), wq, wk, wv, qk_g)


def _mla_attn_kernel(qi_ref, ki_ref, last_ref, q_ref, k_ref, v_ref, o_ref, m_ref, l_ref, acc_ref, *, tq, tk):
    p_id = pl.program_id(1)
    qi = qi_ref[p_id]
    ki = ki_ref[p_id]

    @pl.when(ki == 0)
    def _():
        _init_softmax_state(m_ref, l_ref, acc_ref)

    def attend(mask):
        score_fns = [functools.partial(lambda sl: _dot_nt(k_ref[:, sl], q_ref[:, sl]),
                                       slice(h * MLA_PAD, (h + 1) * MLA_PAD)) for h in range(N_HEADS)]
        values = [v_ref[:, h * HEAD_W:(h + 1) * HEAD_W] for h in range(N_HEADS)]
        _attend(score_fns, values, mask, m_ref, l_ref, acc_ref)

    @pl.when(ki == qi)
    def _():
        attend(_chunk_mask_t(qi, ki, tq, tk))

    @pl.when(ki != qi)
    def _():
        attend(None)

    @pl.when(last_ref[p_id] == 1)
    def _():
        for h in range(N_HEADS):
            o_ref[:, h * HEAD_W:(h + 1) * HEAD_W] = (acc_ref[h] / l_ref[h]).T.astype(o_ref.dtype)


def _mla_attn(qm, km, vm, *, batch, seq, tq=1024, tk=1024):
    assert tq == tk
    n = qm.shape[0]
    hw = N_HEADS * MLA_PAD
    nq, nk = seq // tq, seq // tk
    qi, ki, last = _causal_pairs(nq, tq, tk)
    grid_spec = pltpu.PrefetchScalarGridSpec(
        num_scalar_prefetch=3,
        grid=(batch, int(qi.shape[0])),
        in_specs=[pl.BlockSpec((tq, hw), lambda b, p_, qi_, ki_, la_: (b * nq + qi_[p_], 0)),
                  pl.BlockSpec((tk, hw), lambda b, p_, qi_, ki_, la_: (b * nk + ki_[p_], 0)),
                  pl.BlockSpec((tk, BRANCH_W), lambda b, p_, qi_, ki_, la_: (b * nk + ki_[p_], 0))],
        out_specs=pl.BlockSpec((tq, BRANCH_W), lambda b, p_, qi_, ki_, la_: (b * nq + qi_[p_], 0)),
        scratch_shapes=[pltpu.VMEM((N_HEADS, 1, tq), F32),
                        pltpu.VMEM((N_HEADS, 1, tq), F32),
                        pltpu.VMEM((N_HEADS, HEAD_W, tq), F32)],
    )
    return pl.pallas_call(
        functools.partial(_mla_attn_kernel, tq=tq, tk=tk),
        out_shape=jax.ShapeDtypeStruct((n, BRANCH_W), BF16),
        grid_spec=grid_spec,
        compiler_params=_cparams("parallel", "arbitrary"),
        name="mla_attn",
    )(qi, ki, last, qm, km, vm)


def _merge_kernel(ya_ref, yb_ref, yc_ref, yd_ref, g0_ref, g1_ref, g2_ref, g3_ref, wb_ref, o_ref):
    acc = None
    for n, (y_ref, g_ref) in enumerate(((ya_ref, g0_ref), (yb_ref, g1_ref), (yc_ref, g2_ref), (yd_ref, g3_ref))):
        up = jnp.dot(y_ref[...], wb_ref[n], preferred_element_type=F32)
        term = jax.nn.sigmoid(g_ref[...].astype(F32)) * up
        acc = term if acc is None else acc + term
    o_ref[...] = acc.astype(o_ref.dtype)


def _merge(ys, gates, wb, tm=1024, tn=1024):
    n = gates.shape[0]
    nj = D_MODEL // tn
    y_spec = pl.BlockSpec((tm, BRANCH_W), lambda i, j: (i, 0))
    g_specs = [pl.BlockSpec((tm, tn), functools.partial(lambda i, j, nn: (i, nn * nj + j), nn=nn)) for nn in range(4)]
    return pl.pallas_call(
        _merge_kernel,
        out_shape=jax.ShapeDtypeStruct((n, D_MODEL), BF16),
        grid=(n // tm, nj),
        in_specs=[y_spec] * 4 + g_specs + [pl.BlockSpec((4, BRANCH_W, tn), lambda i, j: (0, 0, j))],
        out_specs=pl.BlockSpec((tm, tn), lambda i, j: (i, j)),
        compiler_params=_cparams("parallel", "arbitrary"),
        name="merge",
    )(*ys, gates, gates, gates, gates, wb)


def _resid_matmul_kernel(m_ref, w_ref, x_ref, mod_ref, o_ref, *, gate_row):
    y = jnp.dot(m_ref[...], w_ref[...], preferred_element_type=F32)
    o_ref[...] = x_ref[...] + mod_ref[gate_row:gate_row + 1, :] * y


def _resid_matmul(m, w, x, mod, *, seq, gate_row, tm=2048, tn=512):
    n, d = x.shape
    kdim = m.shape[1]
    return pl.pallas_call(
        functools.partial(_resid_matmul_kernel, gate_row=gate_row),
        out_shape=jax.ShapeDtypeStruct((n, d), F32),
        grid=(n // tm, d // tn),
        in_specs=[pl.BlockSpec((tm, kdim), lambda i, j: (i, 0)),
                  pl.BlockSpec((kdim, tn), lambda i, j: (0, j)),
                  pl.BlockSpec((tm, tn), lambda i, j: (i, j)),
                  pl.BlockSpec((None, 6, tn), lambda i, j: (i * tm // seq, 0, j))],
        out_specs=pl.BlockSpec((tm, tn), lambda i, j: (i, j)),
        compiler_params=_cparams("parallel", "arbitrary"),
        name="out_proj",
    )(m, w, x, mod)


def _top1_rounds(problems, rounds):
    state = []
    for s, ids in problems:
        row = lax.broadcasted_iota(jnp.int32, s.shape, 0).astype(F32)
        state.append([s, ids, row, float(s.shape[0]), [], []])
    for _ in range(rounds):
        ms = [jnp.max(st[0], axis=0, keepdims=True) for st in state]
        poss = [jnp.min(jnp.where(st[0] == m, st[2], st[3]), axis=0, keepdims=True) for st, m in zip(state, ms)]
        for st, m, pos in zip(state, ms, poss):
            sel = st[2] == pos
            st[4].append(m)
            st[5].append(pos if st[1] is None else jnp.sum(jnp.where(sel, st[1], 0), axis=0, keepdims=True))
            st[0] = jnp.where(sel, NEG_INF, st[0])
    res = []
    for s, ids, row, nrow, vals, outs in state:
        out = jnp.concatenate(outs, axis=0)
        res.append((jnp.concatenate(vals, axis=0), out.astype(jnp.int32) if ids is None else out))
    return res


_CAND_ROWS = [(a, PEER_TOPK // (a + 1)) for a in range(PEER_TOPK)]
_CAND_PAD = -sum(nb for _, nb in _CAND_ROWS) % 8


def _route_kernel(q_ref, keys_ref, e_ref, g_ref, *, tt):
    subs = [slice(sub * 128, (sub + 1) * 128) for sub in range(tt // 128)]
    scores = []
    for rows in subs:
        for c in range(2):
            qc = q_ref[rows, c * 128:(c + 1) * 128]
            sc = lax.dot_general(keys_ref[c], qc, (((1,), (1,)), ((), ())), preferred_element_type=F32,
                                 precision=lax.Precision.HIGHEST)
            scores.append((sc, None))
    tops = _top1_rounds(scores, PEER_TOPK)
    cands = []
    for j in range(len(subs)):
        (s0, i0), (s1, i1) = tops[2 * j], tops[2 * j + 1]
        pad_s = [jnp.full((_CAND_PAD, 128), NEG_INF, F32)] if _CAND_PAD else []
        pad_i = [jnp.zeros((_CAND_PAD, 128), jnp.int32)] if _CAND_PAD else []
        cand_s = jnp.concatenate([s0[a:a + 1] + s1[:nb] for a, nb in _CAND_ROWS] + pad_s, axis=0)
        cand_i = jnp.concatenate([i0[a:a + 1] * PEER_KEYS + i1[:nb] for a, nb in _CAND_ROWS] + pad_i, axis=0)
        cands.append((cand_s, cand_i))
    for rows, (best_s, experts) in zip(subs, _top1_rounds(cands, PEER_TOPK)):
        ex = jnp.exp(best_s - jnp.max(best_s, axis=0, keepdims=True))
        e_ref[:, rows] = experts
        g_ref[:, rows] = ex / jnp.sum(ex, axis=0, keepdims=True)


def _peer_route(q, sub_keys, tt=1024):
    n = q.shape[0]
    e, g = pl.pallas_call(
        functools.partial(_route_kernel, tt=tt),
        out_shape=[jax.ShapeDtypeStruct((PEER_HEADS, PEER_TOPK, n), jnp.int32),
                   jax.ShapeDtypeStruct((PEER_HEADS, PEER_TOPK, n), F32)],
        grid=(n // tt, PEER_HEADS),
        in_specs=[pl.BlockSpec((tt, 256), lambda i, h: (i, h)),
                  pl.BlockSpec((None, 2, PEER_KEYS, 128), lambda i, h: (h, 0, 0, 0))],
        out_specs=[pl.BlockSpec((None, PEER_TOPK, tt), lambda i, h: (h, 0, i)),
                   pl.BlockSpec((None, PEER_TOPK, tt), lambda i, h: (h, 0, i))],
        compiler_params=_cparams("parallel", "parallel"),
        name="peer_route",
    )(q, sub_keys)
    return e.reshape(PEER_SEL, n), g.reshape(PEER_SEL, n)


PEER_TB = 128
PEER_NBUF = 4
D_CHUNKS = D_MODEL // 128


def _peer_apply_kernel(ex_ref, gt_ref, h_ref, x_ref, mod_ref, uv_ref, o_ref, buf_ref, h32_ref, sem):
    h32_ref[...] = h_ref[...].astype(F32)

    def issue(t, slot):
        for k in range(PEER_SEL):
            row = pl.multiple_of(ex_ref[t, k] * D_CHUNKS, D_CHUNKS)
            pltpu.make_async_copy(uv_ref.at[pl.ds(row, D_CHUNKS)], buf_ref.at[slot, :, k],
                                  sem.at[slot]).start(priority=k % 2)

    def wait(slot):
        for r in range(D_CHUNKS):
            pltpu.make_async_copy(uv_ref.at[pl.ds(0, PEER_SEL)], buf_ref.at[slot, r], sem.at[slot]).wait()

    def chunk(slot, r):
        return buf_ref[slot, r]

    lane = lax.broadcasted_iota(jnp.int32, (1, PEER_TB), 1)

    def compute(t, slot):
        acc = jnp.zeros((PEER_SEL, 128), F32)
        for r in range(D_CHUNKS):
            u = pltpu.bitcast(chunk(slot, r) & jnp.uint32(0xFFFF0000), F32)
            acc = acc + u * h32_ref[t, r:r + 1, :]
        s = jnp.sum(acc, axis=1, keepdims=True)
        gate = jnp.sum(jnp.where(lane == t, gt_ref[...], 0.0), axis=1, keepdims=True)
        act = _gelu(s) * gate
        outs = []
        for r in range(D_CHUNKS):
            v = pltpu.bitcast(chunk(slot, r) << 16, F32)
            outs.append(jnp.sum(v * act, axis=0, keepdims=True))
        out = jnp.concatenate(outs, axis=1)
        o_ref[pl.ds(t, 1), :] = x_ref[pl.ds(t, 1), :] + mod_ref[5:6, :] * out

    for slot in range(PEER_NBUF):
        issue(slot, slot)

    def body(i, carry):
        for slot in range(PEER_NBUF):
            t = i * PEER_NBUF + slot
            wait(slot)
            compute(t, slot)

            @pl.when(t + PEER_NBUF < PEER_TB)
            def _():
                issue(t + PEER_NBUF, slot)
        return carry

    lax.fori_loop(0, PEER_TB // PEER_NBUF, body, 0)


def _peer_apply(experts, gates_t, h2, x, mod, uv, *, seq):
    n, d = x.shape
    tb = PEER_TB
    return pl.pallas_call(
        _peer_apply_kernel,
        out_shape=jax.ShapeDtypeStruct((n, d), F32),
        grid=(n // tb,),
        in_specs=[pl.BlockSpec((tb, PEER_SEL), lambda i: (i, 0), memory_space=pltpu.SMEM),
                  pl.BlockSpec((PEER_SEL, tb), lambda i: (0, i)),
                  pl.BlockSpec((tb, D_CHUNKS, 128), lambda i: (i, 0, 0)),
                  pl.BlockSpec((tb, d), lambda i: (i, 0)),
                  pl.BlockSpec((None, 6, d), lambda i: (i * tb // seq, 0, 0)),
                  pl.BlockSpec(memory_space=pl.ANY)],
        out_specs=pl.BlockSpec((tb, d), lambda i: (i, 0)),
        scratch_shapes=[pltpu.VMEM((PEER_NBUF, D_CHUNKS, PEER_SEL, 128), jnp.uint32),
                        pltpu.VMEM((tb, D_CHUNKS, 128), F32),
                        pltpu.SemaphoreType.DMA((PEER_NBUF,))],
        compiler_params=_cparams("arbitrary"),
        name="peer_apply",
    )(experts, gates_t, h2.reshape(n, D_CHUNKS, 128), x, mod, uv)


def _pack_uv_kernel(u_ref, v_ref, o_ref):
    ub = pltpu.bitcast(u_ref[...].astype(BF16).astype(F32), jnp.uint32)
    vb = pltpu.bitcast(v_ref[...].astype(BF16).astype(F32), jnp.uint32)
    w = ub | (vb >> 16)
    for r in range(D_CHUNKS):
        o_ref[:, r, :] = w[:, r * 128:(r + 1) * 128]


def _pack_uv(u_tab, v_tab, tb=512):
    e, d = u_tab.shape
    out = pl.pallas_call(
        _pack_uv_kernel,
        out_shape=jax.ShapeDtypeStruct((e, D_CHUNKS, 128), jnp.uint32),
        grid=(e // tb,),
        in_specs=[pl.BlockSpec((tb, d), lambda i: (i, 0))] * 2,
        out_specs=pl.BlockSpec((tb, D_CHUNKS, 128), lambda i: (i, 0, 0)),
        compiler_params=_cparams("parallel"),
        name="pack_uv",
    )(u_tab, v_tab)
    return out.reshape(e * D_CHUNKS, 128)


def _main_proj_weights(w_in):
    d = w_in.shape[0]
    gate_lo = COL_CQ * BRANCH_W
    latent_lo = gate_lo + 2 * N_HEADS
    merge_lo = latent_lo + MLA_Q_RANK + MLA_KV_RANK + MLA_ROPE
    main = jnp.concatenate([w_in[:, :gate_lo], w_in[:, latent_lo:merge_lo], w_in[:, gate_lo:latent_lo],
                            jnp.zeros((d, P_WIDTH - merge_lo), w_in.dtype)], axis=1)
    return main.astype(BF16), w_in[:, merge_lo:].astype(BF16)


def _mla_weights(w_uq, w_ukv):
    qr = w_uq.shape[0]
    kvr = w_ukv.shape[0]
    wq = jnp.pad(w_uq.reshape(qr, N_HEADS, MLA_QK), ((0, 0), (0, 0), (0, MLA_PAD - MLA_QK))).reshape(qr, -1)
    kv = w_ukv.reshape(kvr, N_HEADS, MLA_NOPE + HEAD_W)
    k_nope = jnp.pad(kv[:, :, :MLA_NOPE], ((0, 0), (0, 0), (MLA_ROPE, MLA_PAD - MLA_QK))).reshape(kvr, -1)
    eye = jnp.pad(jnp.eye(MLA_ROPE, dtype=w_ukv.dtype), ((0, 128 - MLA_ROPE), (0, MLA_PAD - MLA_ROPE)))
    wk = jnp.concatenate([k_nope, jnp.tile(eye, (1, N_HEADS))], axis=0)
    wv = kv[:, :, MLA_NOPE:].reshape(kvr, -1)
    return wq.astype(BF16), wk.astype(BF16), wv.astype(BF16)


def kernel(x, c, positions, norm1_g, norm2_g, w_mod, b_mod, w_in, da_q_norm, da_k_norm, da_lambda, da_out_norm, sg_v_norm, sg_w, sg_b, ml_conv_w, ml_conv_b, ml_gate_b, ml_out_norm, mla_q_norm, mla_kv_norm, mla_w_uq, mla_w_ukv, mla_qk_norm, w_branch, w_out, peer_w_q, peer_sub_keys, peer_u, peer_v):
    batch, seq, d = x.shape
    depth = w_mod.shape[0]
    n = batch * seq
    xf = x.reshape(n, d)

    mod_all = _modulation(c, w_mod, b_mod)
    pos_col = positions.reshape(n, 1).astype(F32)
    cos_da, sin_da = _rope_tables(pos_col, _lane_freq_sign(DA_QK, DA_ROT))
    cos_ml, sin_ml = _rope_tables(pos_col, _lane_freq_sign(128, MLA_ROPE))

    for l in range(depth):
        mod = mod_all[l]
        lambda_init = 0.8 - 0.6 * math.exp(-0.3 * l)
        w_main, w_gates = _main_proj_weights(w_in[l])
        h1 = _norm_mod(xf, norm1_g[l], mod, seq=seq, sc_row=1, sh_row=0)
        p = _matmul(h1, w_main, out_dtype=F32, tm=1024, tn=P_WIDTH // 4, name="in_proj")
        gates = _matmul(h1, w_gates, out_dtype=BF16, tm=1024, tn=2048, name="gate_proj")

        qn, kn = _da_prep(p, cos_da, sin_da, da_q_norm[l], da_k_norm[l])
        ya = _da_attn(qn, kn, p, da_lambda[l], da_out_norm[l], batch=batch, seq=seq, lambda_init=lambda_init)

        yb = _spatial_gating(p, sg_v_norm[l], sg_w[l], sg_b[l])

        qk = _ml_conv(p, ml_conv_w[l], ml_conv_b[l], seq=seq)
        yc = _mlstm(qk, p, ml_gate_b[l], ml_out_norm[l], batch=batch, seq=seq)

        wq, wk, wv = _mla_weights(mla_w_uq[l], mla_w_ukv[l])
        qk_g = jnp.pad(mla_qk_norm[l], ((0, 0), (0, MLA_PAD - MLA_QK)))
        qm, km, vm = _mla_prep(p, cos_ml, sin_ml, mla_q_norm[l], mla_kv_norm[l], wq, wk, wv, qk_g)
        yd = _mla_attn(qm, km, vm, batch=batch, seq=seq)

        mixed = _merge((ya, yb, yc, yd), gates, w_branch[l].astype(BF16))
        xf = _resid_matmul(mixed, w_out[l].astype(BF16), xf, mod, seq=seq, gate_row=2)

        h2 = _norm_mod(xf, norm2_g[l], mod, seq=seq, sc_row=4, sh_row=3)
        q_peer = _matmul(h2, peer_w_q[l].astype(BF16), out_dtype=F32, tm=1024, tn=2048, name="peer_query")
        experts_t, gates_t = _peer_route(q_peer, peer_sub_keys[l])
        uv = _pack_uv(peer_u[l], peer_v[l])
        xf = _peer_apply(experts_t.T, gates_t, h2, xf, mod, uv, seq=seq)

    return xf.reshape(batch, seq, d)
```

```python
import functools
import math

import numpy as np
import jax
import jax.numpy as jnp
from jax import lax
from jax.experimental import pallas as pl
from jax.experimental.pallas import tpu as pltpu

F32 = jnp.float32
BF16 = jnp.bfloat16

D_MODEL = 2048
CHUNK = 64
ROPE_THETA = 500000.0
RMS_EPS = 1e-6
N_HEADS = 4
HEAD_W = 128
BRANCH_W = N_HEADS * HEAD_W
DA_QK = 64
DA_ROT = 16
ML_CONV = 4
MLA_ROPE = 64
MLA_NOPE = 128
MLA_QK = MLA_ROPE + MLA_NOPE
MLA_PAD = 256
MLA_Q_RANK = 512
MLA_KV_RANK = 256
PEER_HEADS = 8
PEER_KEYS = 128
PEER_TOPK = 16
PEER_SEL = PEER_HEADS * PEER_TOPK

P_WIDTH = 11 * 512
COL_DAQ, COL_DAK, COL_DAV, COL_SGU, COL_SGV, COL_MLQK, COL_MLV, COL_MLO, COL_CQ, COL_MISC = 0, 1, 2, 3, 4, 5, 7, 8, 9, 10
MISC_KR = 256
MISC_GATE = 320

VMEM_LIMIT = 56 * 1024 * 1024
NEG_INF = float("-inf")
LOG2_E = math.log2(math.e)


def _cparams(*sem):
    return pltpu.CompilerParams(dimension_semantics=sem, vmem_limit_bytes=VMEM_LIMIT)


def _gelu(x):
    return 0.5 * x * (1.0 + lax.erf(x * (1.0 / math.sqrt(2.0))))


def _dot_nt(a, b):
    return lax.dot_general(a, b, (((1,), (1,)), ((), ())), preferred_element_type=F32)


def _mod_kernel(c_ref, w_ref, b_ref, o_ref):
    c = c_ref[...]
    cond = c * jax.nn.sigmoid(c)
    o_ref[0] = jnp.dot(cond, w_ref[0], preferred_element_type=F32,
                       precision=lax.Precision.HIGHEST) + b_ref[0]


def _modulation(c, w_mod, b_mod):
    depth, d, w6 = w_mod.shape
    b = c.shape[0]
    rows = 8
    cp = jnp.zeros((rows, d), F32).at[:b].set(c)
    tn = 1024
    out = pl.pallas_call(
        _mod_kernel,
        out_shape=jax.ShapeDtypeStruct((depth, rows, w6), F32),
        grid=(depth, w6 // tn),
        in_specs=[pl.BlockSpec((rows, d), lambda l, j: (0, 0)),
                  pl.BlockSpec((1, d, tn), lambda l, j: (l, 0, j)),
                  pl.BlockSpec((1, 1, tn), lambda l, j: (l, 0, j))],
        out_specs=pl.BlockSpec((1, rows, tn), lambda l, j: (l, 0, j)),
        compiler_params=_cparams("parallel", "parallel"),
        name="modulation",
    )(cp, w_mod, b_mod.reshape(depth, 1, w6))
    return out[:, :b].reshape(depth, b, 6, d)


def _norm_mod_kernel(x_ref, g_ref, mod_ref, h_ref, *, sc_row, sh_row):
    x = x_ref[...]
    ms = jnp.mean(x * x, axis=-1, keepdims=True)
    y = x * lax.rsqrt(ms + RMS_EPS) * g_ref[...]
    h_ref[...] = (y * (1.0 + mod_ref[sc_row:sc_row + 1, :]) + mod_ref[sh_row:sh_row + 1, :]).astype(h_ref.dtype)


def _norm_mod(x, g, mod, *, seq, sc_row, sh_row, tm=1024):
    n, d = x.shape
    return pl.pallas_call(
        functools.partial(_norm_mod_kernel, sc_row=sc_row, sh_row=sh_row),
        out_shape=jax.ShapeDtypeStruct((n, d), BF16),
        grid=(n // tm,),
        in_specs=[pl.BlockSpec((tm, d), lambda i: (i, 0)),
                  pl.BlockSpec((1, d), lambda i: (0, 0)),
                  pl.BlockSpec((None, 6, d), lambda i: (i * tm // seq, 0, 0))],
        out_specs=pl.BlockSpec((tm, d), lambda i: (i, 0)),
        compiler_params=_cparams("parallel"),
        name="norm_mod",
    )(x, g.reshape(1, d), mod)


def _matmul_kernel(h_ref, w_ref, o_ref):
    o_ref[...] = jnp.dot(h_ref[...], w_ref[...], preferred_element_type=F32).astype(o_ref.dtype)


def _matmul(h, w, *, out_dtype, tm, tn, name):
    n, d = h.shape
    wn = w.shape[1]
    return pl.pallas_call(
        _matmul_kernel,
        out_shape=jax.ShapeDtypeStruct((n, wn), out_dtype),
        grid=(n // tm, wn // tn),
        in_specs=[pl.BlockSpec((tm, d), lambda i, j: (i, 0)),
                  pl.BlockSpec((d, tn), lambda i, j: (0, j))],
        out_specs=pl.BlockSpec((tm, tn), lambda i, j: (i, j)),
        compiler_params=_cparams("parallel", "arbitrary"),
        name=name,
    )(h, w)


def _rope_table_kernel(pos_ref, f_ref, cos_ref, sin_ref):
    ang = pos_ref[...] * f_ref[0:1, :]
    cos_ref[...] = jnp.cos(ang)
    sin_ref[...] = jnp.sin(ang) * f_ref[1:2, :]


def _rope_tables(pos_col, freq_sign, tm=1024):
    n = pos_col.shape[0]
    return pl.pallas_call(
        _rope_table_kernel,
        out_shape=[jax.ShapeDtypeStruct((n, 128), F32)] * 2,
        grid=(n // tm,),
        in_specs=[pl.BlockSpec((tm, 1), lambda i: (i, 0)),
                  pl.BlockSpec((2, 128), lambda i: (0, 0))],
        out_specs=[pl.BlockSpec((tm, 128), lambda i: (i, 0))] * 2,
        compiler_params=_cparams("parallel"),
        name="rope_tables",
    )(pos_col, freq_sign)


def _lane_freq_sign(group, rot):
    half = rot // 2
    freq = ROPE_THETA ** (-jnp.arange(half, dtype=F32) / half)
    lane = np.arange(128) % group
    idx = np.where(lane < rot, lane % half, 0)
    f = jnp.where(jnp.asarray(lane < rot), freq[idx], 0.0)
    sign = np.where(lane < half, -1.0, np.where(lane < rot, 1.0, 0.0)).astype(np.float32)
    return jnp.stack([f, jnp.asarray(sign)], axis=0)


def _rope_apply(x, cos, sin, group, rot):
    half = rot // 2
    lane = lax.broadcasted_iota(jnp.int32, (1, 128), 1) % group
    partner = jnp.where(lane < half, pltpu.roll(x, 128 - half, 1), pltpu.roll(x, half, 1))
    return x * cos + partner * sin


def _da_prep_kernel(q_ref, k_ref, cos_ref, sin_ref, qg_ref, kg_ref, qo_ref, ko_ref):
    cos = cos_ref[...]
    sin = sin_ref[...]
    lo = lax.broadcasted_iota(jnp.int32, (1, 128), 1) < DA_QK

    def prep(ref, g_ref, o_ref, scale):
        for h in range(N_HEADS):
            x = ref[:, h * HEAD_W:(h + 1) * HEAD_W]
            sq = x * x
            s_lo = jnp.sum(jnp.where(lo, sq, 0.0), axis=-1, keepdims=True)
            s_hi = jnp.sum(jnp.where(lo, 0.0, sq), axis=-1, keepdims=True)
            ms = jnp.where(lo, s_lo, s_hi) * (1.0 / DA_QK)
            y = x * lax.rsqrt(ms + RMS_EPS) * g_ref[...]
            y = _rope_apply(y, cos, sin, DA_QK, DA_ROT)
            o_ref[:, h * HEAD_W:(h + 1) * HEAD_W] = (y * scale).astype(BF16)

    prep(q_ref, qg_ref, qo_ref, DA_QK ** -0.5 * LOG2_E)
    prep(k_ref, kg_ref, ko_ref, 1.0)


def _da_prep(p, cos, sin, q_g, k_g, tm=1024):
    n = p.shape[0]
    return pl.pallas_call(
        _da_prep_kernel,
        out_shape=[jax.ShapeDtypeStruct((n, BRANCH_W), BF16)] * 2,
        grid=(n // tm,),
        in_specs=[pl.BlockSpec((tm, BRANCH_W), lambda i: (i, COL_DAQ)),
                  pl.BlockSpec((tm, BRANCH_W), lambda i: (i, COL_DAK)),
                  pl.BlockSpec((tm, 128), lambda i: (i, 0)),
                  pl.BlockSpec((tm, 128), lambda i: (i, 0)),
                  pl.BlockSpec((1, 128), lambda i: (0, 0)),
                  pl.BlockSpec((1, 128), lambda i: (0, 0))],
        out_specs=[pl.BlockSpec((tm, BRANCH_W), lambda i: (i, 0))] * 2,
        compiler_params=_cparams("parallel"),
        name="da_prep",
    )(p, p, cos, sin, q_g.reshape(1, 128), k_g.reshape(1, 128))


def _causal_pairs(nq, tq, tk):
    qi, ki, last = [], [], []
    for q in range(nq):
        nk = -(-((q + 1) * tq) // tk)
        for k in range(nk):
            qi.append(q)
            ki.append(k)
            last.append(int(k == nk - 1))
    return (jnp.asarray(qi, jnp.int32), jnp.asarray(ki, jnp.int32), jnp.asarray(last, jnp.int32))


def _attend(score_fns, values, mask, m_ref, l_ref, acc_ref):
    n = len(score_fns)
    scores, probs = {}, {}
    for i in range(n + 2):
        if i < n:
            s = score_fns[i]()
            scores[i] = s if mask is None else jnp.where(mask, s, NEG_INF)
        j = i - 1
        if 0 <= j < n:
            s = scores.pop(j)
            m_prev = m_ref[j]
            m_new = jnp.maximum(m_prev, jnp.max(s, axis=0, keepdims=True))
            alpha = jnp.exp2(m_prev - m_new)
            p = jnp.exp2(s - m_new)
            l_ref[j] = alpha * l_ref[j] + jnp.sum(p, axis=0, keepdims=True)
            m_ref[j] = m_new
            probs[j] = (p.astype(BF16), alpha)
        j = i - 2
        if 0 <= j < n:
            p, alpha = probs.pop(j)
            pv = lax.dot_general(values[j], p, (((0,), (0,)), ((), ())), preferred_element_type=F32)
            acc_ref[j] = alpha * acc_ref[j] + pv


def _chunk_mask_t(qi, ki, tq, tk):
    key = ki * tk + lax.broadcasted_iota(jnp.int32, (tk, 1), 0)
    qry = qi * tq + lax.broadcasted_iota(jnp.int32, (1, tq), 1)
    return (key // CHUNK) <= (qry // CHUNK)


def _init_softmax_state(m_ref, l_ref, acc_ref):
    m_ref[...] = jnp.full(m_ref.shape, NEG_INF, F32)
    l_ref[...] = jnp.zeros(l_ref.shape, F32)
    acc_ref[...] = jnp.zeros(acc_ref.shape, F32)


def _da_attn_kernel(qi_ref, ki_ref, last_ref, q_ref, k_ref, v_ref, lam_ref, og_ref, o_ref,
                    m_ref, l_ref, acc_ref, *, lambda_init, tq, tk):
    p_id = pl.program_id(1)
    qi = qi_ref[p_id]
    ki = ki_ref[p_id]

    @pl.when(ki == 0)
    def _():
        _init_softmax_state(m_ref, l_ref, acc_ref)

    lo = lax.broadcasted_iota(jnp.int32, (1, HEAD_W), 1) < DA_QK

    def attend(mask):
        score_fns, values = [], []
        for h in range(N_HEADS):
            sl = slice(h * HEAD_W, (h + 1) * HEAD_W)
            v = v_ref[:, sl].astype(BF16)
            for c in range(2):
                def score(sl=sl, c=c):
                    q = q_ref[:, sl]
                    zero = jnp.zeros_like(q)
                    return _dot_nt(k_ref[:, sl], jnp.where(lo, q, zero) if c == 0 else jnp.where(lo, zero, q))
                score_fns.append(score)
                values.append(v)
        _attend(score_fns, values, mask, m_ref, l_ref, acc_ref)

    @pl.when(ki == qi)
    def _():
        attend(_chunk_mask_t(qi, ki, tq, tk))

    @pl.when(ki != qi)
    def _():
        attend(None)

    @pl.when(last_ref[p_id] == 1)
    def _():
        lf = lam_ref[...]
        lam = (jnp.exp(jnp.sum(lf[0:1] * lf[1:2], axis=-1, keepdims=True))
               - jnp.exp(jnp.sum(lf[2:3] * lf[3:4], axis=-1, keepdims=True)) + lambda_init)
        for h in range(N_HEADS):
            o = acc_ref[2 * h] / l_ref[2 * h] - lam * (acc_ref[2 * h + 1] / l_ref[2 * h + 1])
            ms = jnp.mean(o * o, axis=0, keepdims=True)
            y = o * lax.rsqrt(ms + RMS_EPS) * og_ref[...] * (1.0 - lambda_init)
            o_ref[:, h * HEAD_W:(h + 1) * HEAD_W] = y.T.astype(o_ref.dtype)


def _da_attn(qn, kn, p, lam_vecs, out_g, *, batch, seq, lambda_init, tq=1024, tk=1024):
    assert tq == tk
    n = qn.shape[0]
    nq, nk = seq // tq, seq // tk
    qi, ki, last = _causal_pairs(nq, tq, tk)
    grid_spec = pltpu.PrefetchScalarGridSpec(
        num_scalar_prefetch=3,
        grid=(batch, int(qi.shape[0])),
        in_specs=[pl.BlockSpec((tq, BRANCH_W), lambda b, p_, qi_, ki_, la_: (b * nq + qi_[p_], 0)),
                  pl.BlockSpec((tk, BRANCH_W), lambda b, p_, qi_, ki_, la_: (b * nk + ki_[p_], 0)),
                  pl.BlockSpec((tk, BRANCH_W), lambda b, p_, qi_, ki_, la_: (b * nk + ki_[p_], COL_DAV)),
                  pl.BlockSpec((4, DA_QK), lambda b, p_, qi_, ki_, la_: (0, 0)),
                  pl.BlockSpec((HEAD_W, 1), lambda b, p_, qi_, ki_, la_: (0, 0))],
        out_specs=pl.BlockSpec((tq, BRANCH_W), lambda b, p_, qi_, ki_, la_: (b * nq + qi_[p_], 0)),
        scratch_shapes=[pltpu.VMEM((2 * N_HEADS, 1, tq), F32),
                        pltpu.VMEM((2 * N_HEADS, 1, tq), F32),
                        pltpu.VMEM((2 * N_HEADS, HEAD_W, tq), F32)],
    )
    return pl.pallas_call(
        functools.partial(_da_attn_kernel, lambda_init=lambda_init, tq=tq, tk=tk),
        out_shape=jax.ShapeDtypeStruct((n, BRANCH_W), BF16),
        grid_spec=grid_spec,
        compiler_params=_cparams("parallel", "arbitrary"),
        name="da_attn",
    )(qi, ki, last, qn, kn, p, lam_vecs, out_g.reshape(HEAD_W, 1))


def _sg_kernel(u_ref, v_ref, vg_ref, w_ref, b_ref, o_ref, *, tm):
    idx_i = lax.broadcasted_iota(jnp.int32, (128, 128), 0)
    idx_j = lax.broadcasted_iota(jnp.int32, (128, 128), 1)
    wmask = (idx_j // CHUNK) <= (idx_i // CHUNK)
    for g in range(N_HEADS):
        sl = slice(g * HEAD_W, (g + 1) * HEAD_W)
        u = _gelu(u_ref[:, sl])
        v = _gelu(v_ref[:, sl])
        ms = jnp.mean(v * v, axis=-1, keepdims=True)
        vn = (v * lax.rsqrt(ms + RMS_EPS) * vg_ref[g:g + 1, :]).astype(BF16)
        w = jnp.where(wmask, w_ref[g], 0.0).astype(BF16)
        bias = b_ref[:, g:g + 1]
        for nb in range(tm // 128):
            rows = slice(nb * 128, (nb + 1) * 128)
            sv = jnp.dot(w, vn[rows], preferred_element_type=F32) + bias
            o_ref[rows, sl] = (u[rows] * sv).astype(o_ref.dtype)


def _spatial_gating(p, v_g, w_s, b_s, tm=1024):
    n = p.shape[0]
    return pl.pallas_call(
        functools.partial(_sg_kernel, tm=tm),
        out_shape=jax.ShapeDtypeStruct((n, BRANCH_W), BF16),
        grid=(n // tm,),
        in_specs=[pl.BlockSpec((tm, BRANCH_W), lambda i: (i, COL_SGU)),
                  pl.BlockSpec((tm, BRANCH_W), lambda i: (i, COL_SGV)),
                  pl.BlockSpec((N_HEADS, 128), lambda i: (0, 0)),
                  pl.BlockSpec((N_HEADS, 128, 128), lambda i: (0, 0, 0)),
                  pl.BlockSpec((128, N_HEADS), lambda i: (0, 0))],
        out_specs=pl.BlockSpec((tm, BRANCH_W), lambda i: (i, 0)),
        compiler_params=_cparams("parallel"),
        name="spatial_gating",
    )(p, p, v_g, w_s, b_s.T)


def _ml_conv_kernel(x_ref, halo_ref, w_ref, b_ref, o_ref, *, tm, seq):
    i = pl.program_id(0)
    first = (i * tm) % seq == 0
    halo = jnp.where(first, 0.0, halo_ref[...])
    xc = jnp.concatenate([halo, x_ref[...]], axis=0)
    acc = jnp.zeros((tm, x_ref.shape[1]), F32) + b_ref[...]
    for j in range(ML_CONV):
        shift = ML_CONV - 1 - j
        xs = xc if shift == 0 else pltpu.roll(xc, shift, 0)
        acc = acc + xs[8:8 + tm] * w_ref[j:j + 1, :]
    o_ref[...] = acc * jax.nn.sigmoid(acc)


def _ml_conv(p, conv_w, conv_b, *, seq, tm=2048):
    n = p.shape[0]
    width = 2 * BRANCH_W
    cb = width // BRANCH_W
    return pl.pallas_call(
        functools.partial(_ml_conv_kernel, tm=tm, seq=seq),
        out_shape=jax.ShapeDtypeStruct((n, width), F32),
        grid=(n // tm, cb),
        in_specs=[pl.BlockSpec((tm, BRANCH_W), lambda i, j: (i, COL_MLQK + j)),
                  pl.BlockSpec((8, BRANCH_W), lambda i, j: (jnp.maximum(i * (tm // 8) - 1, 0), COL_MLQK + j)),
                  pl.BlockSpec((ML_CONV, BRANCH_W), lambda i, j: (0, j)),
                  pl.BlockSpec((1, BRANCH_W), lambda i, j: (0, j))],
        out_specs=pl.BlockSpec((tm, BRANCH_W), lambda i, j: (i, j)),
        compiler_params=_cparams("parallel", "parallel"),
        name="ml_conv",
    )(p, p, conv_w, conv_b.reshape(1, width))


def _log_sigmoid(x):
    return jnp.minimum(x, 0.0) - jnp.log1p(jnp.exp(-jnp.abs(x)))


def _mlstm_kernel(q_ref, k_ref, v_ref, o_ref, gc_ref, gr_ref, bc_ref, br_ref, og_ref, y_ref,
                  c_ref, n_ref, m_ref, *, batch):
    @pl.when(pl.program_id(0) == 0)
    def _():
        c_ref[...] = jnp.zeros(c_ref.shape, F32)
        n_ref[...] = jnp.zeros(n_ref.shape, F32)
        m_ref[...] = jnp.zeros(m_ref.shape, F32)

    L = CHUNK
    it = lax.broadcasted_iota(jnp.int32, (L, L), 0)
    is_ = lax.broadcasted_iota(jnp.int32, (L, L), 1)
    tril = is_ <= it
    triu = it <= is_
    chains = [(b, h) for b in range(batch) for h in range(N_HEADS)]
    ins = []
    for b, h in chains:
        st = b * N_HEADS + h
        sl = slice(h * HEAD_W, (h + 1) * HEAD_W)
        ins.append((q_ref[b, :, sl], k_ref[b, :, sl], v_ref[b, :, sl], o_ref[b, :, sl],
                    gc_ref[b, :, MISC_GATE:MISC_GATE + 2 * N_HEADS] + bc_ref[...], gr_ref[b] + br_ref[...],
                    c_ref[st], n_ref[st], m_ref[st][:, 0:1]))
    each = lambda f, *cols: [f(*xs) for xs in zip(*cols)]
    hs = [h for _, h in chains]
    q32, k, v, o, gcol, grow, c_prev, n_prev, m_prev = (list(col) for col in zip(*ins))
    q = each(lambda x: x.astype(BF16), q32)
    k = each(lambda x: x * (HEAD_W ** -0.5), k)
    kb = each(lambda x: x.astype(BF16), k)
    vb = each(lambda x: x.astype(BF16), v)
    li_row = each(lambda gr, h: gr[h:h + 1, :], grow, hs)
    lf_row = each(lambda gr, h: _log_sigmoid(gr[N_HEADS + h:N_HEADS + h + 1, :]), grow, hs)
    li_col = each(lambda gc, h: gc[:, h:h + 1], gcol, hs)
    lf_col = each(lambda gc, h: _log_sigmoid(gc[:, N_HEADS + h:N_HEADS + h + 1]), gcol, hs)
    bcum_col = each(lambda x: jnp.sum(jnp.where(tril, x, 0.0), axis=1, keepdims=True), lf_row)
    bcum_row = each(lambda x: jnp.sum(jnp.where(triu, x, 0.0), axis=0, keepdims=True), lf_col)
    g = each(lambda x: jnp.sum(x, axis=1, keepdims=True), lf_row)
    dmat = each(lambda bc, br, li: jnp.where(tril, bc - br + li, NEG_INF), bcum_col, bcum_row, li_row)
    m_inter = each(lambda bc, m: bc + m, bcum_col, m_prev)
    m_t = each(lambda mi, d: jnp.maximum(mi, jnp.max(d, axis=1, keepdims=True)), m_inter, dmat)
    a_inter = each(lambda mi, mt: jnp.exp(mi - mt), m_inter, m_t)
    qk = each(_dot_nt, q, kb)
    sm = each(lambda d, mt, s_: jnp.exp(d - mt) * s_, dmat, m_t, qk)
    qc = each(lambda q_, c: jnp.dot(q_, c.astype(BF16), preferred_element_type=F32), q, c_prev)
    sv = each(lambda s_, v_: jnp.dot(s_.astype(BF16), v_, preferred_element_type=F32), sm, vb)
    num = each(lambda ai, x, y_: ai * x + y_, a_inter, qc, sv)
    qn = each(lambda q_, n_: jnp.sum(q_ * n_, axis=1, keepdims=True), q32, n_prev)
    den = each(lambda ai, x, s_: ai * x + jnp.sum(s_, axis=1, keepdims=True), a_inter, qn, sm)
    h_t = each(lambda nu, de, mt: nu / jnp.maximum(jnp.abs(de), jnp.exp(-mt)), num, den, m_t)
    dec_row = each(lambda g_, br, li: g_ - br + li, g, bcum_row, li_row)
    dec_col = each(lambda g_, bc, li: g_ - bc + li, g, bcum_col, li_col)
    m_new = each(lambda g_, m, d: jnp.maximum(g_ + m, jnp.max(d, axis=1, keepdims=True)), g, m_prev, dec_row)
    w_col = each(lambda d, m: jnp.exp(d - m), dec_col, m_new)
    a = each(lambda g_, m, mn: jnp.exp(g_ + m - mn), g, m_prev, m_new)
    kw = each(lambda w, k_: w * k_, w_col, k)
    kv = each(lambda kw_, v_: lax.dot_general(kw_.astype(BF16), v_, (((0,), (0,)), ((), ())),
                                              preferred_element_type=F32), kw, vb)
    c_new = each(lambda a_, c, x: a_ * c + x, a, c_prev, kv)
    n_new = each(lambda a_, n_, kw_: a_ * n_ + jnp.sum(kw_, axis=0, keepdims=True), a, n_prev, kw)
    y = each(lambda o_, h_: jax.nn.sigmoid(o_) * h_, o, h_t)
    ms = each(lambda y_: jnp.mean(y_ * y_, axis=-1, keepdims=True), y)
    y = each(lambda y_, ms_, h: (y_ * lax.rsqrt(ms_ + RMS_EPS) * og_ref[h:h + 1, :]).astype(y_ref.dtype), y, ms, hs)
    for (b, h), c_, n_, m_, y_ in zip(chains, c_new, n_new, m_new, y):
        st = b * N_HEADS + h
        c_ref[st] = c_
        n_ref[st] = n_
        m_ref[st] = jnp.broadcast_to(m_, (1, HEAD_W))
        y_ref[b, :, h * HEAD_W:(h + 1) * HEAD_W] = y_


def _mlstm(qk, p, gate_b, out_g, *, batch, seq):
    n = p.shape[0]
    nc = seq // CHUNK
    g8 = p[:, COL_MISC * 512 + MISC_GATE:COL_MISC * 512 + MISC_GATE + 2 * N_HEADS]
    grow = jnp.swapaxes(g8.reshape(batch, nc, CHUNK, 2 * N_HEADS), 2, 3)
    bias = gate_b.reshape(1, 2 * N_HEADS)
    qk3 = qk.reshape(batch, seq, qk.shape[1])
    p3 = p.reshape(batch, seq, p.shape[1])
    blk = lambda col: pl.BlockSpec((batch, CHUNK, BRANCH_W), functools.partial(lambda c, col_: (0, c, col_), col_=col))
    y = pl.pallas_call(
        functools.partial(_mlstm_kernel, batch=batch),
        out_shape=jax.ShapeDtypeStruct((batch, seq, BRANCH_W), BF16),
        grid=(nc,),
        in_specs=[blk(0), blk(1), blk(COL_MLV), blk(COL_MLO), blk(COL_MISC),
                  pl.BlockSpec((batch, None, 2 * N_HEADS, CHUNK), lambda c: (0, c, 0, 0)),
                  pl.BlockSpec((1, 2 * N_HEADS), lambda c: (0, 0)),
                  pl.BlockSpec((2 * N_HEADS, 1), lambda c: (0, 0)),
                  pl.BlockSpec((N_HEADS, HEAD_W), lambda c: (0, 0))],
        out_specs=pl.BlockSpec((batch, CHUNK, BRANCH_W), lambda c: (0, c, 0)),
        scratch_shapes=[pltpu.VMEM((batch * N_HEADS, HEAD_W, HEAD_W), F32),
                        pltpu.VMEM((batch * N_HEADS, 1, HEAD_W), F32),
                        pltpu.VMEM((batch * N_HEADS, 1, HEAD_W), F32)],
        compiler_params=_cparams("arbitrary"),
        name="mlstm",
    )(qk3, qk3, p3, p3, p3, grow, bias, bias.reshape(2 * N_HEADS, 1), out_g)
    return y.reshape(n, BRANCH_W)


def _mla_prep_kernel(cq_ref, misc_ref, cos_ref, sin_ref, qa_ref, kva_ref, wq_ref, wk_ref, wv_ref, qkg_ref,
                     qo_ref, ko_ref, vo_ref):
    def rms(x, g):
        ms = jnp.mean(x * x, axis=-1, keepdims=True)
        return x * lax.rsqrt(ms + RMS_EPS) * g

    cqn = rms(cq_ref[...], qa_ref[...]).astype(BF16)
    ckvn = rms(misc_ref[:, :MLA_KV_RANK], kva_ref[...]).astype(BF16)
    lane = lax.broadcasted_iota(jnp.int32, (1, 128), 1)
    kr = jnp.where(lane < MLA_ROPE, misc_ref[:, MISC_KR:MISC_KR + 128], 0.0).astype(BF16)
    q_lin = jnp.dot(cqn, wq_ref[...], preferred_element_type=F32)
    k_lin = jnp.dot(jnp.concatenate([ckvn, kr], axis=1), wk_ref[...], preferred_element_type=F32)
    vo_ref[...] = jnp.dot(ckvn, wv_ref[...], preferred_element_type=F32).astype(vo_ref.dtype)
    cos = cos_ref[...]
    sin = sin_ref[...]

    def finish(x_lin, g_row, o_ref, scale):
        for h in range(N_HEADS):
            x = x_lin[:, h * MLA_PAD:(h + 1) * MLA_PAD]
            ms = jnp.sum(x * x, axis=-1, keepdims=True) * (1.0 / MLA_QK)
            y = x * lax.rsqrt(ms + RMS_EPS) * qkg_ref[g_row:g_row + 1, :]
            y0 = _rope_apply(y[:, :128], cos, sin, 128, MLA_ROPE)
            o_ref[:, h * MLA_PAD:h * MLA_PAD + 128] = (y0 * scale).astype(o_ref.dtype)
            o_ref[:, h * MLA_PAD + 128:(h + 1) * MLA_PAD] = (y[:, 128:] * scale).astype(o_ref.dtype)

    finish(q_lin, 0, qo_ref, MLA_QK ** -0.5 * LOG2_E)
    finish(k_lin, 1, ko_ref, 1.0)


def _mla_prep(p, cos, sin, q_a_g, kv_a_g, wq, wk, wv, qk_g, tm=1024):
    n = p.shape[0]
    hw = N_HEADS * MLA_PAD
    full = lambda shape: pl.BlockSpec(shape, lambda i: (0,) * len(shape))
    return pl.pallas_call(
        _mla_prep_kernel,
        out_shape=[jax.ShapeDtypeStruct((n, hw), BF16), jax.ShapeDtypeStruct((n, hw), BF16),
                   jax.ShapeDtypeStruct((n, BRANCH_W), BF16)],
        grid=(n // tm,),
        in_specs=[pl.BlockSpec((tm, 512), lambda i: (i, COL_CQ)),
                  pl.BlockSpec((tm, 512), lambda i: (i, COL_MISC)),
                  pl.BlockSpec((tm, 128), lambda i: (i, 0)),
                  pl.BlockSpec((tm, 128), lambda i: (i, 0)),
                  full((1, MLA_Q_RANK)), full((1, MLA_KV_RANK)),
                  full(wq.shape), full(wk.shape), full(wv.shape), full((2, MLA_PAD))],
        out_specs=[pl.BlockSpec((tm, hw), lambda i: (i, 0)), pl.BlockSpec((tm, hw), lambda i: (i, 0)),
                   pl.BlockSpec((tm, BRANCH_W), lambda i: (i, 0))],
        compiler_params=_cparams("parallel"),
        name="mla_prep",
    )(p, p, cos, sin, q_a_g.reshape(1, -1), kv_a_g.reshape(1, -1), wq, wk, wv, qk_g)


def _mla_attn_kernel(qi_ref, ki_ref, last_ref, q_ref, k_ref, v_ref, o_ref, m_ref, l_ref, acc_ref, *, tq, tk):
    p_id = pl.program_id(1)
    qi = qi_ref[p_id]
    ki = ki_ref[p_id]

    @pl.when(ki == 0)
    def _():
        _init_softmax_state(m_ref, l_ref, acc_ref)

    def attend(mask):
        score_fns = [functools.partial(lambda sl: _dot_nt(k_ref[:, sl], q_ref[:, sl]),
                                       slice(h * MLA_PAD, (h + 1) * MLA_PAD)) for h in range(N_HEADS)]
        values = [v_ref[:, h * HEAD_W:(h + 1) * HEAD_W] for h in range(N_HEADS)]
        _attend(score_fns, values, mask, m_ref, l_ref, acc_ref)

    @pl.when(ki == qi)
    def _():
        attend(_chunk_mask_t(qi, ki, tq, tk))

    @pl.when(ki != qi)
    def _():
        attend(None)

    @pl.when(last_ref[p_id] == 1)
    def _():
        for h in range(N_HEADS):
            o_ref[:, h * HEAD_W:(h + 1) * HEAD_W] = (acc_ref[h] / l_ref[h]).T.astype(o_ref.dtype)


def _mla_attn(qm, km, vm, *, batch, seq, tq=1024, tk=1024):
    assert tq == tk
    n = qm.shape[0]
    hw = N_HEADS * MLA_PAD
    nq, nk = seq // tq, seq // tk
    qi, ki, last = _causal_pairs(nq, tq, tk)
    grid_spec = pltpu.PrefetchScalarGridSpec(
        num_scalar_prefetch=3,
        grid=(batch, int(qi.shape[0])),
        in_specs=[pl.BlockSpec((tq, hw), lambda b, p_, qi_, ki_, la_: (b * nq + qi_[p_], 0)),
                  pl.BlockSpec((tk, hw), lambda b, p_, qi_, ki_, la_: (b * nk + ki_[p_], 0)),
                  pl.BlockSpec((tk, BRANCH_W), lambda b, p_, qi_, ki_, la_: (b * nk + ki_[p_], 0))],
        out_specs=pl.BlockSpec((tq, BRANCH_W), lambda b, p_, qi_, ki_, la_: (b * nq + qi_[p_], 0)),
        scratch_shapes=[pltpu.VMEM((N_HEADS, 1, tq), F32),
                        pltpu.VMEM((N_HEADS, 1, tq), F32),
                        pltpu.VMEM((N_HEADS, HEAD_W, tq), F32)],
    )
    return pl.pallas_call(
        functools.partial(_mla_attn_kernel, tq=tq, tk=tk),
        out_shape=jax.ShapeDtypeStruct((n, BRANCH_W), BF16),
        grid_spec=grid_spec,
        compiler_params=_cparams("parallel", "arbitrary"),
        name="mla_attn",
    )(qi, ki, last, qm, km, vm)


def _merge_kernel(ya_ref, yb_ref, yc_ref, yd_ref, g0_ref, g1_ref, g2_ref, g3_ref, wb_ref, o_ref):
    acc = None
    for n, (y_ref, g_ref) in enumerate(((ya_ref, g0_ref), (yb_ref, g1_ref), (yc_ref, g2_ref), (yd_ref, g3_ref))):
        up = jnp.dot(y_ref[...], wb_ref[n], preferred_element_type=F32)
        term = jax.nn.sigmoid(g_ref[...].astype(F32)) * up
        acc = term if acc is None else acc + term
    o_ref[...] = acc.astype(o_ref.dtype)


def _merge(ys, gates, wb, tm=1024, tn=1024):
    n = gates.shape[0]
    nj = D_MODEL // tn
    y_spec = pl.BlockSpec((tm, BRANCH_W), lambda i, j: (i, 0))
    g_specs = [pl.BlockSpec((tm, tn), functools.partial(lambda i, j, nn: (i, nn * nj + j), nn=nn)) for nn in range(4)]
    return pl.pallas_call(
        _merge_kernel,
        out_shape=jax.ShapeDtypeStruct((n, D_MODEL), BF16),
        grid=(n // tm, nj),
        in_specs=[y_spec] * 4 + g_specs + [pl.BlockSpec((4, BRANCH_W, tn), lambda i, j: (0, 0, j))],
        out_specs=pl.BlockSpec((tm, tn), lambda i, j: (i, j)),
        compiler_params=_cparams("parallel", "arbitrary"),
        name="merge",
    )(*ys, gates, gates, gates, gates, wb)


def _resid_matmul_kernel(m_ref, w_ref, x_ref, mod_ref, o_ref, *, gate_row):
    y = jnp.dot(m_ref[...], w_ref[...], preferred_element_type=F32)
    o_ref[...] = x_ref[...] + mod_ref[gate_row:gate_row + 1, :] * y


def _resid_matmul(m, w, x, mod, *, seq, gate_row, tm=2048, tn=512):
    n, d = x.shape
    kdim = m.shape[1]
    return pl.pallas_call(
        functools.partial(_resid_matmul_kernel, gate_row=gate_row),
        out_shape=jax.ShapeDtypeStruct((n, d), F32),
        grid=(n // tm, d // tn),
        in_specs=[pl.BlockSpec((tm, kdim), lambda i, j: (i, 0)),
                  pl.BlockSpec((kdim, tn), lambda i, j: (0, j)),
                  pl.BlockSpec((tm, tn), lambda i, j: (i, j)),
                  pl.BlockSpec((None, 6, tn), lambda i, j: (i * tm // seq, 0, j))],
        out_specs=pl.BlockSpec((tm, tn), lambda i, j: (i, j)),
        compiler_params=_cparams("parallel", "arbitrary"),
        name="out_proj",
    )(m, w, x, mod)


def _top1_rounds(problems, rounds):
    state = []
    for s, ids in problems:
        row = lax.broadcasted_iota(jnp.int32, s.shape, 0).astype(F32)
        state.append([s, ids, row, float(s.shape[0]), [], []])
    for _ in range(rounds):
        ms = [jnp.max(st[0], axis=0, keepdims=True) for st in state]
        poss = [jnp.min(jnp.where(st[0] == m, st[2], st[3]), axis=0, keepdims=True) for st, m in zip(state, ms)]
        for st, m, pos in zip(state, ms, poss):
            sel = st[2] == pos
            st[4].append(m)
            st[5].append(pos if st[1] is None else jnp.sum(jnp.where(sel, st[1], 0), axis=0, keepdims=True))
            st[0] = jnp.where(sel, NEG_INF, st[0])
    res = []
    for s, ids, row, nrow, vals, outs in state:
        out = jnp.concatenate(outs, axis=0)
        res.append((jnp.concatenate(vals, axis=0), out.astype(jnp.int32) if ids is None else out))
    return res


_CAND_ROWS = [(a, PEER_TOPK // (a + 1)) for a in range(PEER_TOPK)]
_CAND_PAD = -sum(nb for _, nb in _CAND_ROWS) % 8


def _route_kernel(q_ref, keys_ref, e_ref, g_ref, *, tt):
    subs = [slice(sub * 128, (sub + 1) * 128) for sub in range(tt // 128)]
    scores = []
    for rows in subs:
        for c in range(2):
            qc = q_ref[rows, c * 128:(c + 1) * 128]
            sc = lax.dot_general(keys_ref[c], qc, (((1,), (1,)), ((), ())), preferred_element_type=F32,
                                 precision=lax.Precision.HIGHEST)
            scores.append((sc, None))
    tops = _top1_rounds(scores, PEER_TOPK)
    cands = []
    for j in range(len(subs)):
        (s0, i0), (s1, i1) = tops[2 * j], tops[2 * j + 1]
        pad_s = [jnp.full((_CAND_PAD, 128), NEG_INF, F32)] if _CAND_PAD else []
        pad_i = [jnp.zeros((_CAND_PAD, 128), jnp.int32)] if _CAND_PAD else []
        cand_s = jnp.concatenate([s0[a:a + 1] + s1[:nb] for a, nb in _CAND_ROWS] + pad_s, axis=0)
        cand_i = jnp.concatenate([i0[a:a + 1] * PEER_KEYS + i1[:nb] for a, nb in _CAND_ROWS] + pad_i, axis=0)
        cands.append((cand_s, cand_i))
    for rows, (best_s, experts) in zip(subs, _top1_rounds(cands, PEER_TOPK)):
        ex = jnp.exp(best_s - jnp.max(best_s, axis=0, keepdims=True))
        e_ref[:, rows] = experts
        g_ref[:, rows] = ex / jnp.sum(ex, axis=0, keepdims=True)


def _peer_route(q, sub_keys, tt=1024):
    n = q.shape[0]
    e, g = pl.pallas_call(
        functools.partial(_route_kernel, tt=tt),
        out_shape=[jax.ShapeDtypeStruct((PEER_HEADS, PEER_TOPK, n), jnp.int32),
                   jax.ShapeDtypeStruct((PEER_HEADS, PEER_TOPK, n), F32)],
        grid=(n // tt, PEER_HEADS),
        in_specs=[pl.BlockSpec((tt, 256), lambda i, h: (i, h)),
                  pl.BlockSpec((None, 2, PEER_KEYS, 128), lambda i, h: (h, 0, 0, 0))],
        out_specs=[pl.BlockSpec((None, PEER_TOPK, tt), lambda i, h: (h, 0, i)),
                   pl.BlockSpec((None, PEER_TOPK, tt), lambda i, h: (h, 0, i))],
        compiler_params=_cparams("parallel", "parallel"),
        name="peer_route",
    )(q, sub_keys)
    return e.reshape(PEER_SEL, n), g.reshape(PEER_SEL, n)


PEER_TB = 128
PEER_NBUF = 4
D_CHUNKS = D_MODEL // 128


def _peer_apply_kernel(ex_ref, gt_ref, h_ref, x_ref, mod_ref, uv_ref, o_ref, buf_ref, h32_ref, sem):
    h32_ref[...] = h_ref[...].astype(F32)

    def issue(t, slot):
        for k in range(PEER_SEL):
            row = pl.multiple_of(ex_ref[t, k] * D_CHUNKS, D_CHUNKS)
            pltpu.make_async_copy(uv_ref.at[pl.ds(row, D_CHUNKS)], buf_ref.at[slot, :, k],
                                  sem.at[slot]).start(priority=k % 2)

    def wait(slot):
        for r in range(D_CHUNKS):
            pltpu.make_async_copy(uv_ref.at[pl.ds(0, PEER_SEL)], buf_ref.at[slot, r], sem.at[slot]).wait()

    def chunk(slot, r):
        return buf_ref[slot, r]

    lane = lax.broadcasted_iota(jnp.int32, (1, PEER_TB), 1)

    def compute(t, slot):
        acc = jnp.zeros((PEER_SEL, 128), F32)
        for r in range(D_CHUNKS):
            u = pltpu.bitcast(chunk(slot, r) & jnp.uint32(0xFFFF0000), F32)
            acc = acc + u * h32_ref[t, r:r + 1, :]
        s = jnp.sum(acc, axis=1, keepdims=True)
        gate = jnp.sum(jnp.where(lane == t, gt_ref[...], 0.0), axis=1, keepdims=True)
        act = _gelu(s) * gate
        outs = []
        for r in range(D_CHUNKS):
            v = pltpu.bitcast(chunk(slot, r) << 16, F32)
            outs.append(jnp.sum(v * act, axis=0, keepdims=True))
        out = jnp.concatenate(outs, axis=1)
        o_ref[pl.ds(t, 1), :] = x_ref[pl.ds(t, 1), :] + mod_ref[5:6, :] * out

    for slot in range(PEER_NBUF):
        issue(slot, slot)

    def body(i, carry):
        for slot in range(PEER_NBUF):
            t = i * PEER_NBUF + slot
            wait(slot)
            compute(t, slot)

            @pl.when(t + PEER_NBUF < PEER_TB)
            def _():
                issue(t + PEER_NBUF, slot)
        return carry

    lax.fori_loop(0, PEER_TB // PEER_NBUF, body, 0)


def _peer_apply(experts, gates_t, h2, x, mod, uv, *, seq):
    n, d = x.shape
    tb = PEER_TB
    return pl.pallas_call(
        _peer_apply_kernel,
        out_shape=jax.ShapeDtypeStruct((n, d), F32),
        grid=(n // tb,),
        in_specs=[pl.BlockSpec((tb, PEER_SEL), lambda i: (i, 0), memory_space=pltpu.SMEM),
                  pl.BlockSpec((PEER_SEL, tb), lambda i: (0, i)),
                  pl.BlockSpec((tb, D_CHUNKS, 128), lambda i: (i, 0, 0)),
                  pl.BlockSpec((tb, d), lambda i: (i, 0)),
                  pl.BlockSpec((None, 6, d), lambda i: (i * tb // seq, 0, 0)),
                  pl.BlockSpec(memory_space=pl.ANY)],
        out_specs=pl.BlockSpec((tb, d), lambda i: (i, 0)),
        scratch_shapes=[pltpu.VMEM((PEER_NBUF, D_CHUNKS, PEER_SEL, 128), jnp.uint32),
                        pltpu.VMEM((tb, D_CHUNKS, 128), F32),
                        pltpu.SemaphoreType.DMA((PEER_NBUF,))],
        compiler_params=_cparams("arbitrary"),
        name="peer_apply",
    )(experts, gates_t, h2.reshape(n, D_CHUNKS, 128), x, mod, uv)


def _pack_uv_kernel(u_ref, v_ref, o_ref):
    ub = pltpu.bitcast(u_ref[...].astype(BF16).astype(F32), jnp.uint32)
    vb = pltpu.bitcast(v_ref[...].astype(BF16).astype(F32), jnp.uint32)
    w = ub | (vb >> 16)
    for r in range(D_CHUNKS):
        o_ref[:, r, :] = w[:, r * 128:(r + 1) * 128]


def _pack_uv(u_tabs, v_tabs, layer, tb=512):
    _, e, d = u_tabs.shape
    out = pl.pallas_call(
        _pack_uv_kernel,
        out_shape=jax.ShapeDtypeStruct((e, D_CHUNKS, 128), jnp.uint32),
        grid=(e // tb,),
        in_specs=[pl.BlockSpec((None, tb, d), lambda i: (layer, i, 0))] * 2,
        out_specs=pl.BlockSpec((tb, D_CHUNKS, 128), lambda i: (i, 0, 0)),
        compiler_params=_cparams("parallel"),
        name="pack_uv",
    )(u_tabs, v_tabs)
    return out.reshape(e * D_CHUNKS, 128)


def _main_proj_weights(w_in):
    d = w_in.shape[0]
    gate_lo = COL_CQ * BRANCH_W
    latent_lo = gate_lo + 2 * N_HEADS
    merge_lo = latent_lo + MLA_Q_RANK + MLA_KV_RANK + MLA_ROPE
    main = jnp.concatenate([w_in[:, :gate_lo], w_in[:, latent_lo:merge_lo], w_in[:, gate_lo:latent_lo],
                            jnp.zeros((d, P_WIDTH - merge_lo), w_in.dtype)], axis=1)
    return main.astype(BF16), w_in[:, merge_lo:].astype(BF16)


def _mla_weights(w_uq, w_ukv):
    qr = w_uq.shape[0]
    kvr = w_ukv.shape[0]
    wq = jnp.pad(w_uq.reshape(qr, N_HEADS, MLA_QK), ((0, 0), (0, 0), (0, MLA_PAD - MLA_QK))).reshape(qr, -1)
    kv = w_ukv.reshape(kvr, N_HEADS, MLA_NOPE + HEAD_W)
    k_nope = jnp.pad(kv[:, :, :MLA_NOPE], ((0, 0), (0, 0), (MLA_ROPE, MLA_PAD - MLA_QK))).reshape(kvr, -1)
    eye = jnp.pad(jnp.eye(MLA_ROPE, dtype=w_ukv.dtype), ((0, 128 - MLA_ROPE), (0, MLA_PAD - MLA_ROPE)))
    wk = jnp.concatenate([k_nope, jnp.tile(eye, (1, N_HEADS))], axis=0)
    wv = kv[:, :, MLA_NOPE:].reshape(kvr, -1)
    return wq.astype(BF16), wk.astype(BF16), wv.astype(BF16)


def kernel(x, c, positions, norm1_g, norm2_g, w_mod, b_mod, w_in, da_q_norm, da_k_norm, da_lambda, da_out_norm, sg_v_norm, sg_w, sg_b, ml_conv_w, ml_conv_b, ml_gate_b, ml_out_norm, mla_q_norm, mla_kv_norm, mla_w_uq, mla_w_ukv, mla_qk_norm, w_branch, w_out, peer_w_q, peer_sub_keys, peer_u, peer_v):
    batch, seq, d = x.shape
    depth = w_mod.shape[0]
    n = batch * seq
    xf = x.reshape(n, d)

    mod_all = _modulation(c, w_mod, b_mod)
    pos_col = positions.reshape(n, 1).astype(F32)
    cos_da, sin_da = _rope_tables(pos_col, _lane_freq_sign(DA_QK, DA_ROT))
    cos_ml, sin_ml = _rope_tables(pos_col, _lane_freq_sign(128, MLA_ROPE))

    for l in range(depth):
        mod = mod_all[l]
        lambda_init = 0.8 - 0.6 * math.exp(-0.3 * l)
        w_main, w_gates = _main_proj_weights(w_in[l])
        h1 = _norm_mod(xf, norm1_g[l], mod, seq=seq, sc_row=1, sh_row=0)
        p = _matmul(h1, w_main, out_dtype=F32, tm=1024, tn=P_WIDTH // 4, name="in_proj")
        gates = _matmul(h1, w_gates, out_dtype=BF16, tm=1024, tn=2048, name="gate_proj")

        qn, kn = _da_prep(p, cos_da, sin_da, da_q_norm[l], da_k_norm[l])
        ya = _da_attn(qn, kn, p, da_lambda[l], da_out_norm[l], batch=batch, seq=seq, lambda_init=lambda_init)

        yb = _spatial_gating(p, sg_v_norm[l], sg_w[l], sg_b[l])

        qk = _ml_conv(p, ml_conv_w[l], ml_conv_b[l], seq=seq)
        yc = _mlstm(qk, p, ml_gate_b[l], ml_out_norm[l], batch=batch, seq=seq)

        wq, wk, wv = _mla_weights(mla_w_uq[l], mla_w_ukv[l])
        qk_g = jnp.pad(mla_qk_norm[l], ((0, 0), (0, MLA_PAD - MLA_QK)))
        qm, km, vm = _mla_prep(p, cos_ml, sin_ml, mla_q_norm[l], mla_kv_norm[l], wq, wk, wv, qk_g)
        yd = _mla_attn(qm, km, vm, batch=batch, seq=seq)

        mixed = _merge((ya, yb, yc, yd), gates, w_branch[l].astype(BF16))
        xf = _resid_matmul(mixed, w_out[l].astype(BF16), xf, mod, seq=seq, gate_row=2)

        h2 = _norm_mod(xf, norm2_g[l], mod, seq=seq, sc_row=4, sh_row=3)
        q_peer = _matmul(h2, peer_w_q[l].astype(BF16), out_dtype=F32, tm=1024, tn=2048, name="peer_query")
        experts_t, gates_t = _peer_route(q_peer, peer_sub_keys[l])
        uv = _pack_uv(peer_u, peer_v, l)
        xf = _peer_apply(experts_t.T, gates_t, h2, xf, mod, uv, seq=seq)

    return xf.reshape(batch, seq, d)
```
